```python
import math
import jax, jax.numpy as jnp
from jax import lax
import numpy as np

D_MODEL = 2048
BATCH = 32
SEQ = 256
DEPTH = 4
DEC_BATCH = 2
DEC_SEQ = 4096
PAST_LEN = 512

GRID_W = 64
N_MIXERS = 2
N_GLA_LAYERS = (DEPTH + 1) // 2
N_SSD_LAYERS = DEPTH // 2

GLA_HEADS = 4
GLA_DK = D_MODEL // 2 // GLA_HEADS
GLA_DV = D_MODEL // GLA_HEADS
GLA_RANK = 16
GLA_TAU = 16.0
GLA_CHUNK = 32
ROPE_THETA = 10000.0
GLA_HK = GLA_HEADS * GLA_DK
GLA_HV = GLA_HEADS * GLA_DV
GLA_IN = 2 * GLA_HK + 2 * GLA_HV + 2 * GLA_RANK

SSD_DINNER = 2 * D_MODEL
SSD_HEADDIM = 64
SSD_HEADS = SSD_DINNER // SSD_HEADDIM
SSD_STATE = 128
SSD_GROUPS = 8
SSD_CONV = 3
SSD_CHUNK = 64
SSD_CONV_DIM = SSD_DINNER + 2 * SSD_GROUPS * SSD_STATE
SSD_IN = SSD_DINNER + SSD_CONV_DIM + 2 * SSD_HEADS

N_EXPERTS = 64
TOP_K = 8
EXPERT_FF = 512
SHARED_FF = 512
N_EXPERT_GROUPS = 8
TOPK_GROUPS = 4
ROUTED_SCALE = 2.5
MOE_BLOCK = 128

DEEPNORM_ALPHA = (2 * DEPTH) ** 0.25
DEEPNORM_BETA = (8 * DEPTH) ** -0.25
LN_EPS = 1e-5
RMS_EPS = 1e-6

kernel_name = 'hybrid_gla_ssd_moe_diffusion_step'

f32 = jnp.float32


def layer_norm(x, g, b):
    xf = x.astype(f32)
    mu = jnp.mean(xf, axis=-1, keepdims=True)
    var = jnp.mean(jnp.square(xf - mu), axis=-1, keepdims=True)
    return ((xf - mu) * lax.rsqrt(var + LN_EPS) * g.astype(f32) + b.astype(f32)).astype(x.dtype)


def rms_norm(x, g):
    xf = x.astype(f32)
    return xf * lax.rsqrt(jnp.mean(jnp.square(xf), axis=-1, keepdims=True) + RMS_EPS) * g.astype(f32)


def ada_params(cond, w_ada, b_ada):
    m = jax.nn.silu(cond) @ w_ada + b_ada
    return [t[:, None, :] for t in jnp.split(m, 6, axis=-1)]


def modulate(x, shift, scale):
    return x * (1.0 + scale) + shift


def post_norm(x, sub_out, gate, g, b):
    return layer_norm(DEEPNORM_ALPHA * x + gate * sub_out, g, b)


def flip_t(a):
    return jnp.flip(a, axis=1)


def grid_rope(rows):
    row = jnp.repeat(jnp.arange(rows, dtype=f32), GRID_W)
    col = jnp.tile(jnp.arange(GRID_W, dtype=f32), rows)
    nf = GLA_DK // 4
    freqs = ROPE_THETA ** (-jnp.arange(nf, dtype=f32) / nf)
    ang = jnp.concatenate([row[:, None] * freqs, col[:, None] * freqs], axis=-1)
    return jnp.cos(ang), jnp.sin(ang)


def apply_rope(x, cos, sin):
    x1, x2 = jnp.split(x.astype(f32), 2, axis=-1)
    c = cos[None, :, None, :]
    s = sin[None, :, None, :]
    return jnp.concatenate([x1 * c - x2 * s, x1 * s + x2 * c], axis=-1).astype(x.dtype)


def gla_scan(q, k, v, log_a, s0):
    bsz, t, h, dk = q.shape
    dv = v.shape[-1]
    c = GLA_CHUNK
    n = t // c
    qf = q.astype(f32).reshape(bsz, n, c, h, dk)
    kf = k.astype(f32).reshape(bsz, n, c, h, dk)
    vf = v.astype(f32).reshape(bsz, n, c, h, dv)
    b = jnp.cumsum(log_a.astype(f32).reshape(bsz, n, c, h, dk), axis=2)
    b_last = b[:, :, -1:]
    q_dec = qf * jnp.exp(b)
    k_inv = kf * jnp.exp(-b)
    k_end = kf * jnp.exp(b_last - b)
    mask = jnp.tril(jnp.ones((c, c), bool))
    scores = jnp.where(mask, jnp.einsum('bnihd,bnjhd->bnhij', q_dec, k_inv), 0.0)
    o_intra = jnp.einsum('bnhij,bnjhv->bnihv', scores, vf)

    def step(s, inp):
        qc, kc, vc, dec = inp
        o = jnp.einsum('bihd,bhdv->bihv', qc, s)
        s = s * dec[..., None] + jnp.einsum('bjhd,bjhv->bhdv', kc, vc)
        return s, o

    sw = lambda a: jnp.swapaxes(a, 0, 1)
    s_fin, o_inter = lax.scan(step, s0.astype(f32),
                              (sw(q_dec), sw(k_end), sw(vf), sw(jnp.exp(b_last[:, :, 0]))))
    o = o_intra + sw(o_inter)
    return o.reshape(bsz, t, h, dv), s_fin


def gla_mixer(h, w_in, w_gate, b_gate, norm_g, w_out, s0_f, s0_b, rope):
    bsz, t, _ = h.shape
    proj = h @ w_in
    q, k, v, r, gf, gb = jnp.split(
        proj, [GLA_HK, 2 * GLA_HK, 2 * GLA_HK + GLA_HV, 2 * GLA_HK + 2 * GLA_HV,
               2 * GLA_HK + 2 * GLA_HV + GLA_RANK], axis=-1)
    q = q.reshape(bsz, t, GLA_HEADS, GLA_DK) * (GLA_DK ** -0.5)
    k = k.reshape(bsz, t, GLA_HEADS, GLA_DK)
    v = v.reshape(bsz, t, GLA_HEADS, GLA_DV)
    if rope is not None:
        q = apply_rope(q, rope[0], rope[1])
        k = apply_rope(k, rope[0], rope[1])
    la_f = (jax.nn.log_sigmoid((gf @ w_gate[0] + b_gate[0]).astype(f32)) / GLA_TAU).reshape(bsz, t, GLA_HEADS, GLA_DK)
    la_b = (jax.nn.log_sigmoid((gb @ w_gate[1] + b_gate[1]).astype(f32)) / GLA_TAU).reshape(bsz, t, GLA_HEADS, GLA_DK)
    o_f, s_f = gla_scan(q, k, v, la_f, s0_f)
    o_b, s_b = gla_scan(flip_t(q), flip_t(k), flip_t(v), flip_t(la_b), s0_b)
    o = o_f + flip_t(o_b)
    o = rms_norm(o, norm_g).reshape(bsz, t, GLA_HV).astype(h.dtype)
    return (o * jax.nn.silu(r)) @ w_out, s_f, s_b


def centred_conv(x, w, b):
    pad = w.shape[0] // 2
    y = lax.conv_general_dilated(x, w[:, None, :], window_strides=(1,), padding=[(pad, pad)],
                                 dimension_numbers=('NWC', 'WIO', 'NWC'),
                                 feature_group_count=x.shape[-1])
    return y + b


def ssd_scan(x, dt, a, bm, cm, h0):
    bsz, t, h, p = x.shape
    g, n_st = bm.shape[2], bm.shape[3]
    r = h // g
    c = SSD_CHUNK
    n = t // c
    xf = x.astype(f32).reshape(bsz, n, c, g, r, p)
    dtf = dt.astype(f32).reshape(bsz, n, c, g, r)
    bf = bm.astype(f32).reshape(bsz, n, c, g, n_st)
    cf = cm.astype(f32).reshape(bsz, n, c, g, n_st)
    cs = jnp.cumsum(dtf * a.astype(f32).reshape(g, r), axis=2)
    cs_t = jnp.moveaxis(cs, 2, -1)
    mask = jnp.tril(jnp.ones((c, c), bool))
    decay = jnp.exp(jnp.where(mask, cs_t[..., :, None] - cs_t[..., None, :], -jnp.inf))
    cb = jnp.einsum('bnigs,bnjgs->bngij', cf, bf)
    m = cb[:, :, :, None] * decay * jnp.moveaxis(dtf, 2, -1)[..., None, :]
    y_intra = jnp.einsum('bngrij,bnjgrp->bnigrp', m, xf)
    cs_last = cs[:, :, -1]
    w_end = jnp.exp(cs_last[:, :, None] - cs) * dtf

    def step(hs, inp):
        c_c, d_in, b_c, we, x_c, dec = inp
        y = jnp.einsum('bigs,bgrps->bigrp', c_c, hs) * d_in[..., None]
        hs = hs * dec[..., None, None] + jnp.einsum('bjgs,bjgr,bjgrp->bgrps', b_c, we, x_c)
        return hs, y

    sw = lambda z: jnp.swapaxes(z, 0, 1)
    h_fin, y_inter = lax.scan(step, h0.astype(f32).reshape(bsz, g, r, p, n_st),
                              (sw(cf), sw(jnp.exp(cs)), sw(bf), sw(w_end), sw(xf), sw(jnp.exp(cs_last))))
    y = y_intra + sw(y_inter)
    return y.reshape(bsz, t, h, p), h_fin.reshape(bsz, h, p, n_st)


def ssd_mixer(h, w_in, conv_w, conv_b, dt_bias, a_log, d_skip, norm_g, w_out, h0_f, h0_b):
    bsz, t, _ = h.shape
    proj = h @ w_in
    z, xbc, dtf, dtb = jnp.split(proj, [SSD_DINNER, SSD_DINNER + SSD_CONV_DIM,
                                        SSD_DINNER + SSD_CONV_DIM + SSD_HEADS], axis=-1)
    xbc = jax.nn.silu(centred_conv(xbc, conv_w, conv_b))
    xs, bm, cm = jnp.split(xbc, [SSD_DINNER, SSD_DINNER + SSD_GROUPS * SSD_STATE], axis=-1)
    xs = xs.reshape(bsz, t, SSD_HEADS, SSD_HEADDIM)
    bm = bm.reshape(bsz, t, SSD_GROUPS, SSD_STATE)
    cm = cm.reshape(bsz, t, SSD_GROUPS, SSD_STATE)
    dt_f = jax.nn.softplus((dtf + dt_bias[0]).astype(f32))
    dt_b = jax.nn.softplus((dtb + dt_bias[1]).astype(f32))
    a_f = -jnp.exp(a_log[0].astype(f32))
    a_b = -jnp.exp(a_log[1].astype(f32))
    y_f, hf = ssd_scan(xs, dt_f, a_f, bm, cm, h0_f)
    y_b, hb = ssd_scan(flip_t(xs), flip_t(dt_b), a_b, flip_t(bm), flip_t(cm), h0_b)
    y = y_f + flip_t(y_b) + d_skip.astype(f32)[:, None] * xs.astype(f32)
    y = y.reshape(bsz, t, SSD_DINNER) * jax.nn.silu(z.astype(f32))
    y = rms_norm(y.reshape(bsz, t, SSD_GROUPS, SSD_DINNER // SSD_GROUPS), jnp.ones((), f32))
    y = (y.reshape(bsz, t, SSD_DINNER) * norm_g.astype(f32)).astype(h.dtype)
    return y @ w_out, hf, hb


def route(x, w_router, router_bias):
    t = x.shape[0]
    scores = jax.nn.sigmoid((x @ w_router).astype(f32))
    choice = scores + router_bias.astype(f32)
    grp = choice.reshape(t, N_EXPERT_GROUPS, N_EXPERTS // N_EXPERT_GROUPS)
    grp_score = jnp.sum(lax.top_k(grp, 2)[0], axis=-1)
    _, top_g = lax.top_k(grp_score, TOPK_GROUPS)
    gmask = jnp.sum(jax.nn.one_hot(top_g, N_EXPERT_GROUPS, dtype=f32), axis=1) > 0
    emask = jnp.repeat(gmask, N_EXPERTS // N_EXPERT_GROUPS, axis=1)
    _, idx = lax.top_k(jnp.where(emask, choice, -jnp.inf), TOP_K)
    w = jnp.take_along_axis(scores, idx, axis=1)
    w = w / jnp.sum(w, axis=-1, keepdims=True) * ROUTED_SCALE
    return idx, w


def routed_experts(x, idx, w, w_gate, w_up, w_down):
    t, d = x.shape
    tk = t * TOP_K
    e_flat = idx.reshape(tk)
    order = jnp.argsort(e_flat, stable=True)
    e_sorted = e_flat[order]
    tok_sorted = order // TOP_K
    w_sorted = w.reshape(tk)[order]
    counts = jnp.bincount(e_flat, length=N_EXPERTS)
    padded = (counts + MOE_BLOCK - 1) // MOE_BLOCK * MOE_BLOCK
    start = jnp.cumsum(counts) - counts
    pend = jnp.cumsum(padded)
    pstart = pend - padded
    dest = pstart[e_sorted] + (jnp.arange(tk) - start[e_sorted])
    n_blocks = -(-(tk + N_EXPERTS * (MOE_BLOCK - 1)) // MOE_BLOCK)
    tok_buf = jnp.zeros((n_blocks * MOE_BLOCK,), jnp.int32).at[dest].set(tok_sorted.astype(jnp.int32))
    blk_expert = jnp.minimum(jnp.searchsorted(pend, jnp.arange(n_blocks) * MOE_BLOCK, side='right'),
                             N_EXPERTS - 1)

    def block(inp):
        toks, e = inp
        xb = x[toks]
        hb = jax.nn.silu(xb @ w_gate[e]) * (xb @ w_up[e])
        return hb @ w_down[e]

    out_buf = lax.map(block, (tok_buf.reshape(n_blocks, MOE_BLOCK), blk_expert)).reshape(-1, d)
    contrib = out_buf[dest].astype(f32) * w_sorted[:, None]
    return jax.ops.segment_sum(contrib, tok_sorted, num_segments=t)


def moe_ffn(h, w_router, router_bias, w_gate, w_up, w_down, ws_gate, ws_up, ws_down):
    bsz, t, d = h.shape
    xf = h.reshape(bsz * t, d)
    idx, w = route(xf, w_router, router_bias)
    routed = routed_experts(xf, idx, w, w_gate, w_up, w_down)
    shared = (jax.nn.silu(xf @ ws_gate) * (xf @ ws_up)) @ ws_down
    return (routed.astype(h.dtype) + shared).reshape(bsz, t, d)


def setup_inputs(seed: int = 0) -> dict:
    key = jax.random.key(seed)
    ks = iter(jax.random.split(key, 48))
    nrm = lambda shape, scale: jax.random.normal(next(ks), shape, f32) * scale
    L = DEPTH
    dt0 = jnp.exp(jax.random.uniform(next(ks), (N_SSD_LAYERS, 2, SSD_HEADS), f32,
                                     math.log(1e-3), math.log(1e-1)))
    return {
        'x_prompt': nrm((BATCH, SEQ, D_MODEL), 1.0),
        'x_sample': nrm((DEC_BATCH, DEC_SEQ, D_MODEL), 1.0),
        'c': nrm((DEC_BATCH, D_MODEL), 1.0),
        'state_gla': nrm((DEC_BATCH, N_GLA_LAYERS, 2, GLA_HEADS, GLA_DK, GLA_DV), 1.0),
        'state_ssd': nrm((DEC_BATCH, N_SSD_LAYERS, 2, SSD_HEADS, SSD_HEADDIM, SSD_STATE), 0.5),
        'c_ctx': nrm((D_MODEL,), 1.0),
        'w_ada': nrm((L, D_MODEL, 6 * D_MODEL), 0.5 * D_MODEL ** -0.5),
        'b_ada': nrm((L, 6 * D_MODEL), 0.02),
        'ln_g': 1.0 + nrm((L, 2, D_MODEL), 0.02),
        'ln_b': nrm((L, 2, D_MODEL), 0.02),
        'gla_w_in': nrm((N_GLA_LAYERS, D_MODEL, GLA_IN), D_MODEL ** -0.5),
        'gla_w_gate': nrm((N_GLA_LAYERS, 2, GLA_RANK, GLA_HK), GLA_RANK ** -0.5),
        'gla_b_gate': nrm((N_GLA_LAYERS, 2, GLA_HK), 0.1),
        'gla_norm_g': 1.0 + nrm((N_GLA_LAYERS, GLA_DV), 0.02),
        'gla_w_out': nrm((N_GLA_LAYERS, GLA_HV, D_MODEL), DEEPNORM_BETA * GLA_HV ** -0.5),
        'ssd_w_in': nrm((N_SSD_LAYERS, D_MODEL, SSD_IN), D_MODEL ** -0.5),
        'ssd_conv_w': nrm((N_SSD_LAYERS, SSD_CONV, SSD_CONV_DIM), SSD_CONV ** -0.5),
        'ssd_conv_b': nrm((N_SSD_LAYERS, SSD_CONV_DIM), 0.02),
        'ssd_dt_bias': dt0 + jnp.log(-jnp.expm1(-dt0)),
        'ssd_a_log': jnp.log(jax.random.uniform(next(ks), (N_SSD_LAYERS, 2, SSD_HEADS), f32, 1.0, 16.0)),
        'ssd_d': 1.0 + nrm((N_SSD_LAYERS, SSD_HEADS), 0.02),
        'ssd_norm_g': 1.0 + nrm((N_SSD_LAYERS, SSD_DINNER), 0.02),
        'ssd_w_out': nrm((N_SSD_LAYERS, SSD_DINNER, D_MODEL), DEEPNORM_BETA * SSD_DINNER ** -0.5),
        'moe_w_router': nrm((L, D_MODEL, N_EXPERTS), D_MODEL ** -0.5),
        'moe_router_bias': nrm((L, N_EXPERTS), 0.01),
        'moe_w_gate': nrm((L, N_EXPERTS, D_MODEL, EXPERT_FF), D_MODEL ** -0.5),
        'moe_w_up': nrm((L, N_EXPERTS, D_MODEL, EXPERT_FF), D_MODEL ** -0.5),
        'moe_w_down': nrm((L, N_EXPERTS, EXPERT_FF, D_MODEL), DEEPNORM_BETA * EXPERT_FF ** -0.5),
        'shared_w_gate': nrm((L, D_MODEL, SHARED_FF), D_MODEL ** -0.5),
        'shared_w_up': nrm((L, D_MODEL, SHARED_FF), D_MODEL ** -0.5),
        'shared_w_down': nrm((L, SHARED_FF, D_MODEL), DEEPNORM_BETA * SHARED_FF ** -0.5),
    }


def reference(x_prompt, x_sample, c, state_gla, state_ssd, c_ctx, w_ada, b_ada, ln_g, ln_b,
              gla_w_in, gla_w_gate, gla_b_gate, gla_norm_g, gla_w_out,
              ssd_w_in, ssd_conv_w, ssd_conv_b, ssd_dt_bias, ssd_a_log, ssd_d, ssd_norm_g, ssd_w_out,
              moe_w_router, moe_router_bias, moe_w_gate, moe_w_up, moe_w_down,
              shared_w_gate, shared_w_up, shared_w_down):
    bp = x_prompt.shape[0]
    rows = x_sample.shape[1] // GRID_W
    rope = grid_rope(rows)
    xp, xs = x_prompt, x_sample
    new_gla, new_ssd = [], []
    for l in range(DEPTH):
        mp_ = ada_params(c_ctx[None, :], w_ada[l], b_ada[l])
        ms_ = ada_params(c, w_ada[l], b_ada[l])
        hp = modulate(xp, mp_[0], mp_[1])
        hs = modulate(xs, ms_[0], ms_[1])
        j = l // N_MIXERS
        if l % N_MIXERS == 0:
            z0 = jnp.zeros((bp, GLA_HEADS, GLA_DK, GLA_DV), f32)
            out_p, sf, sb = gla_mixer(hp, gla_w_in[j], gla_w_gate[j], gla_b_gate[j], gla_norm_g[j],
                                      gla_w_out[j], z0, z0, None)
            out_s, _, _ = gla_mixer(hs, gla_w_in[j], gla_w_gate[j], gla_b_gate[j], gla_norm_g[j],
                                    gla_w_out[j], state_gla[:, j, 0], state_gla[:, j, 1], rope)
            new_gla.append(jnp.stack([sf, sb], axis=1).astype(x_prompt.dtype))
        else:
            z0 = jnp.zeros((bp, SSD_HEADS, SSD_HEADDIM, SSD_STATE), f32)
            out_p, hf, hb = ssd_mixer(hp, ssd_w_in[j], ssd_conv_w[j], ssd_conv_b[j], ssd_dt_bias[j],
                                      ssd_a_log[j], ssd_d[j], ssd_norm_g[j], ssd_w_out[j], z0, z0)
            out_s, _, _ = ssd_mixer(hs, ssd_w_in[j], ssd_conv_w[j], ssd_conv_b[j], ssd_dt_bias[j],
                                    ssd_a_log[j], ssd_d[j], ssd_norm_g[j], ssd_w_out[j],
                                    state_ssd[:, j, 0], state_ssd[:, j, 1])
            new_ssd.append(jnp.stack([hf, hb], axis=1).astype(x_prompt.dtype))
        xp = post_norm(xp, out_p, mp_[2], ln_g[l, 0], ln_b[l, 0])
        xs = post_norm(xs, out_s, ms_[2], ln_g[l, 0], ln_b[l, 0])
        ffn_p = moe_ffn(modulate(xp, mp_[3], mp_[4]), moe_w_router[l], moe_router_bias[l], moe_w_gate[l],
                        moe_w_up[l], moe_w_down[l], shared_w_gate[l], shared_w_up[l], shared_w_down[l])
        ffn_s = moe_ffn(modulate(xs, ms_[3], ms_[4]), moe_w_router[l], moe_router_bias[l], moe_w_gate[l],
                        moe_w_up[l], moe_w_down[l], shared_w_gate[l], shared_w_up[l], shared_w_down[l])
        xp = post_norm(xp, ffn_p, mp_[5], ln_g[l, 1], ln_b[l, 1])
        xs = post_norm(xs, ffn_s, ms_[5], ln_g[l, 1], ln_b[l, 1])
    new_state_gla = jnp.stack(new_gla, axis=1)
    new_state_ssd = jnp.stack(new_ssd, axis=1)
    return (xp, xs, new_state_gla, new_state_ssd)
```

```python
import functools
import math

import jax
import jax.numpy as jnp
from jax import lax
from jax.experimental import pallas as pl
from jax.experimental.pallas import tpu as pltpu

f32 = jnp.float32
bf16 = jnp.bfloat16

GRID_W = 64
GLA_HEADS = 4
GLA_RANK = 16
GLA_TAU = 16.0
GLA_CHUNK = 32
ROPE_THETA = 10000.0
SSD_HEADDIM = 64
SSD_STATE = 128
SSD_GROUPS = 8
N_EXPERT_GROUPS = 8
TOPK_GROUPS = 4
TOP_K = 8
ROUTED_SCALE = 2.5
LN_EPS = 1e-5
RMS_EPS = 1e-6

LANES = 128
VMEM_BYTES_V7X = 64 * 2 ** 20


def _cparams(n_axes, vmem_mib):
    assert vmem_mib * 2 ** 20 < VMEM_BYTES_V7X
    return pltpu.CompilerParams(dimension_semantics=("arbitrary",) * n_axes,
                                vmem_limit_bytes=vmem_mib * 2 ** 20)


def _dot(a, b):
    return jnp.dot(a, b, preferred_element_type=f32)


def _dot_nt(a, b):
    return lax.dot_general(a, b, (((1,), (1,)), ((), ())), preferred_element_type=f32)


def _dot_tn(a, b):
    return lax.dot_general(a, b, (((0,), (0,)), ((), ())), preferred_element_type=f32)


def _split3(x):
    hi = x.astype(bf16)
    r = x - hi.astype(f32)
    mid = r.astype(bf16)
    lo = (r - mid.astype(f32)).astype(bf16)
    return hi, mid, lo


def _dot_exact_lhs(a_bf, x):
    hi, mid, lo = _split3(x)
    return (_dot(a_bf, lo) + _dot(a_bf, mid)) + _dot(a_bf, hi)


def _dot_exact_rhs(x, e_bf):
    hi, mid, lo = _split3(x)
    return (_dot(lo, e_bf) + _dot(mid, e_bf)) + _dot(hi, e_bf)


def _silu(x):
    return x * jax.nn.sigmoid(x)


def _cast_rows(src_ref, dst_ref, rows):
    n = src_ref.shape[0]
    rows = math.gcd(n, rows)

    def body(i, c):
        r = pl.multiple_of(i * rows, rows)
        dst_ref[pl.ds(r, rows), :] = src_ref[pl.ds(r, rows), :].astype(dst_ref.dtype)
        return c

    lax.fori_loop(0, n // rows, body, 0)


def _ada_body(c_ref, w_ref, b_ref, o_ref):
    a = _silu(c_ref[...]).astype(bf16)
    o_ref[0] = _dot(a, w_ref[0].astype(bf16)) + b_ref[0]


def ada_all(cond8, w_ada, b_ada, tn=1024):
    L, D, N = w_ada.shape
    return pl.pallas_call(
        _ada_body,
        grid=(L, N // tn),
        in_specs=[pl.BlockSpec((8, D), lambda l, n: (0, 0)),
                  pl.BlockSpec((1, D, tn), lambda l, n: (l, 0, n)),
                  pl.BlockSpec((1, 1, tn), lambda l, n: (l, 0, n))],
        out_specs=pl.BlockSpec((1, 8, tn), lambda l, n: (l, 0, n)),
        out_shape=jax.ShapeDtypeStruct((L, 8, N), f32),
        compiler_params=_cparams(2, 40),
        name="ada",
    )(cond8, w_ada, b_ada.reshape(L, 1, N))


N_ADA = 6


class Rows:
    def __init__(self, n_prompt, seq_s, n_seq_s):
        self.n_prompt = n_prompt
        self.seq_s = seq_s
        self.mods_per_layer = (1 + n_seq_s) * N_ADA

    def group(self, row0):
        return jnp.where(row0 < self.n_prompt, 0, 1 + (row0 - self.n_prompt) // self.seq_s)


def _mod_index(rows, layer, j, tm):
    assert rows.n_prompt % tm == 0 and rows.seq_s % tm == 0

    def index_map(i):
        return (layer * rows.mods_per_layer + rows.group(i * tm) * N_ADA + j, 0, 0)
    return index_map


def _mm_in_body(x_ref, sh_ref, sc_ref, w_ref, o_ref, wb_ref):
    @pl.when(pl.program_id(1) == 0)
    def _():
        _cast_rows(w_ref, wb_ref, 256)

    h = (x_ref[...] * (1.0 + sc_ref[0]) + sh_ref[0]).astype(bf16)
    o_ref[...] = _dot(h, wb_ref[...])


def mm_in(x, mods, rows, layer, w, col0, n_cols, tn, tm=512):
    nt, d = x.shape
    assert n_cols % tn == 0 and col0 % tn == 0 and nt % tm == 0
    sh = _mod_index(rows, layer, 0, tm)
    sc = _mod_index(rows, layer, 1, tm)
    cb = col0 // tn
    return pl.pallas_call(
        _mm_in_body,
        grid=(n_cols // tn, nt // tm),
        in_specs=[pl.BlockSpec((tm, d), lambda n, m: (m, 0)),
                  pl.BlockSpec((1, 1, d), lambda n, m: sh(m)),
                  pl.BlockSpec((1, 1, d), lambda n, m: sc(m)),
                  pl.BlockSpec((d, tn), lambda n, m: (0, cb + n))],
        out_specs=pl.BlockSpec((tm, tn), lambda n, m: (m, n)),
        out_shape=jax.ShapeDtypeStruct((nt, n_cols), f32),
        scratch_shapes=[pltpu.VMEM((d, tn), bf16)],
        compiler_params=_cparams(2, 48),
        name="mm_in",
    )(x, mods, mods, w)


def _layer_norm(v, g, b):
    mu = jnp.mean(v, axis=-1, keepdims=True)
    d = v - mu
    var = jnp.mean(d * d, axis=-1, keepdims=True)
    return d * lax.rsqrt(var + LN_EPS) * g + b


def _mm_post_body(y_ref, w_ref, x_ref, gate_ref, g_ref, b_ref, o_ref, acc_ref, *, nk, alpha):
    k = pl.program_id(1)
    part = _dot(y_ref[...], w_ref[...])

    @pl.when(k == 0)
    def _():
        acc_ref[...] = part

    @pl.when(k > 0)
    def _():
        acc_ref[...] += part

    @pl.when(k == nk - 1)
    def _():
        v = alpha * x_ref[...] + gate_ref[0] * acc_ref[...]
        o_ref[...] = _layer_norm(v, g_ref[...], b_ref[...])


def mm_post(y, w_bf, x, mods, rows, layer, gate_j, ln_g, ln_b, alpha, tm=256, tk=2048):
    nt, kdim = y.shape
    d = x.shape[1]
    nk = kdim // tk
    gate = _mod_index(rows, layer, gate_j, tm)
    return pl.pallas_call(
        functools.partial(_mm_post_body, nk=nk, alpha=alpha),
        grid=(nt // tm, nk),
        in_specs=[pl.BlockSpec((tm, tk), lambda m, k: (m, k)),
                  pl.BlockSpec((tk, d), lambda m, k: (k, 0)),
                  pl.BlockSpec((tm, d), lambda m, k: (m, 0)),
                  pl.BlockSpec((1, 1, d), lambda m, k: gate(m)),
                  pl.BlockSpec((1, d), lambda m, k: (0, 0)),
                  pl.BlockSpec((1, d), lambda m, k: (0, 0))],
        out_specs=pl.BlockSpec((tm, d), lambda m, k: (m, 0)),
        out_shape=jax.ShapeDtypeStruct((nt, d), f32),
        scratch_shapes=[pltpu.VMEM((tm, d), f32)],
        compiler_params=_cparams(2, 48),
        name="mm_post",
    )(y, w_bf, x, mods, ln_g.reshape(1, d), ln_b.reshape(1, d))


def _chunk_masks(tb, c, reverse):
    i = lax.broadcasted_iota(jnp.int32, (tb, tb), 0)
    j = lax.broadcasted_iota(jnp.int32, (tb, tb), 1)
    same = (i // c) == (j // c)
    tri = (j >= i) if reverse else (j <= i)
    return jnp.logical_and(same, tri), same


def _gla_direction(q, k, v, g, wg, bg, rope, st_ref, reverse, c):
    tb, dk = q.shape
    n_chunks = tb // c
    causal, same = _chunk_masks(tb, c, reverse)
    causal_bf = jnp.where(causal, 1.0, 0.0).astype(bf16)
    same_bf = jnp.where(same, 1.0, 0.0).astype(bf16)

    pre = _dot(g.astype(bf16), wg.astype(bf16)) + bg
    log_a = (jnp.minimum(pre, 0.0) - jnp.log1p(jnp.exp(-jnp.abs(pre)))) * (1.0 / GLA_TAU)
    b = _dot_exact_lhs(causal_bf, log_a)
    total = _dot_exact_lhs(same_bf, log_a)

    q = q * (dk ** -0.5)
    if rope is not None:
        cos, sin = rope
        half = dk // 2
        q = jnp.concatenate([q[:, :half] * cos - q[:, half:] * sin, q[:, :half] * sin + q[:, half:] * cos], axis=1)
        k = jnp.concatenate([k[:, :half] * cos - k[:, half:] * sin, k[:, :half] * sin + k[:, half:] * cos], axis=1)

    q_dec = (q * jnp.exp(b)).astype(bf16)
    k_inv = (k * jnp.exp(-b)).astype(bf16)
    k_end = (k * jnp.exp(total - b)).astype(bf16)
    dec = jnp.exp(total)
    v_bf = v.astype(bf16)

    scores = jnp.where(causal, _dot_nt(q_dec, k_inv), 0.0)
    o_intra = _dot(scores.astype(bf16), v_bf)

    order = range(n_chunks - 1, -1, -1) if reverse else range(n_chunks)
    o_inter = [None] * n_chunks
    st = st_ref[...]
    for ci in order:
        sl = slice(ci * c, (ci + 1) * c)
        o_inter[ci] = _dot_nt(q_dec[sl], st.astype(bf16))
        st = st * dec[ci * c:ci * c + 1, :] + _dot_tn(v_bf[sl], k_end[sl])
    st_ref[...] = st
    return o_intra + jnp.concatenate(o_inter, axis=0)


def _gla_body(*refs, c, use_rope, has_init, emit_state, nb):
    refs = list(refs)
    qf, kf, vf, gf, qb, kb, vb, gb, wgf, wgb, bgf, bgb = refs[:12]
    refs = refs[12:]
    rope_f = rope_b = None
    if use_rope:
        rope_f = (refs[0][...], refs[1][...])
        rope_b = (refs[2][...], refs[3][...])
        refs = refs[4:]
    if has_init:
        s0_ref = refs[0]
        refs = refs[1:]
    of_ref, ob_ref = refs[:2]
    refs = refs[2:]
    if emit_state:
        sfin_ref = refs[0]
        refs = refs[1:]
    stf_ref, stb_ref = refs
    t = pl.program_id(2)

    @pl.when(t == 0)
    def _():
        if has_init:
            stf_ref[...] = s0_ref[0, 0, 0].T
            stb_ref[...] = s0_ref[0, 1, 0].T
        else:
            stf_ref[...] = jnp.zeros_like(stf_ref)
            stb_ref[...] = jnp.zeros_like(stb_ref)

    of_ref[...] = _gla_direction(qf[...], kf[...], vf[...], gf[...], wgf[...], bgf[0], rope_f, stf_ref, False, c)
    ob_ref[...] = _gla_direction(qb[...], kb[...], vb[...], gb[...], wgb[...], bgb[0], rope_b, stb_ref, True, c)

    if emit_state:
        @pl.when(t == nb - 1)
        def _():
            sfin_ref[0, 0, 0] = stf_ref[...].T
            sfin_ref[0, 1, 0] = stb_ref[...].T


def gla_scan(proj, g, wg_pad, b_gate, row0, n_seq, seq_len, rope=None, s0=None, emit_state=False, tb=256):
    h = GLA_HEADS
    hk = wg_pad.shape[2]
    dk = hk // h
    dv = 2 * dk
    nb = seq_len // tb
    base = row0 // tb
    assert seq_len % tb == 0 and row0 % tb == 0 and dv == 2 * dk

    def fblk(s, t):
        return base + s * nb + t

    def bblk(s, t):
        return base + s * nb + (nb - 1 - t)

    v_off = 2 * hk // dv
    in_specs, args = [], []
    for blk in (fblk, bblk):
        in_specs += [pl.BlockSpec((tb, dk), lambda s, hh, t, blk=blk: (blk(s, t), hh)),
                     pl.BlockSpec((tb, dk), lambda s, hh, t, blk=blk: (blk(s, t), h + hh)),
                     pl.BlockSpec((tb, dv), lambda s, hh, t, blk=blk: (blk(s, t), v_off + hh)),
                     pl.BlockSpec((tb, LANES), lambda s, hh, t, blk=blk: (blk(s, t), 0))]
        args += [proj, proj, proj, g]
    in_specs += [pl.BlockSpec((None, LANES, dk), lambda s, hh, t: (0, 0, hh)),
                 pl.BlockSpec((None, LANES, dk), lambda s, hh, t: (1, 0, hh)),
                 pl.BlockSpec((None, 1, dk), lambda s, hh, t: (0, 0, hh)),
                 pl.BlockSpec((None, 1, dk), lambda s, hh, t: (1, 0, hh))]
    bg3 = b_gate.reshape(2, 1, hk)
    args += [wg_pad, wg_pad, bg3, bg3]
    if rope is not None:
        for tab_blk in (lambda s, hh, t: (t, 0), lambda s, hh, t: (nb - 1 - t, 0)):
            in_specs += [pl.BlockSpec((tb, dk // 2), tab_blk), pl.BlockSpec((tb, dk // 2), tab_blk)]
            args += [rope[0], rope[1]]
    if s0 is not None:
        in_specs.append(pl.BlockSpec((1, 2, 1, dk, dv), lambda s, hh, t: (s, 0, hh, 0, 0)))
        args.append(s0)
    n_rows = n_seq * seq_len
    out_specs = [pl.BlockSpec((tb, dv), lambda s, hh, t: (s * nb + t, hh)),
                 pl.BlockSpec((tb, dv), lambda s, hh, t: (s * nb + (nb - 1 - t), hh))]
    out_shape = [jax.ShapeDtypeStruct((n_rows, h * dv), f32)] * 2
    if emit_state:
        out_specs.append(pl.BlockSpec((1, 2, 1, dk, dv), lambda s, hh, t: (s, 0, hh, 0, 0)))
        out_shape.append(jax.ShapeDtypeStruct((n_seq, 2, h, dk, dv), f32))
    return pl.pallas_call(
        functools.partial(_gla_body, c=GLA_CHUNK, use_rope=rope is not None, has_init=s0 is not None,
                          emit_state=emit_state, nb=nb),
        grid=(n_seq, h, nb),
        in_specs=in_specs,
        out_specs=out_specs,
        out_shape=out_shape,
        scratch_shapes=[pltpu.VMEM((dv, dk), f32), pltpu.VMEM((dv, dk), f32)],
        compiler_params=_cparams(3, 40),
        name="gla_scan",
    )(*args)


def _gla_gate_body(of_ref, ob_ref, r_ref, g_ref, y_ref):
    o = of_ref[...] + ob_ref[...]
    o = o * lax.rsqrt(jnp.mean(o * o, axis=-1, keepdims=True) + RMS_EPS) * g_ref[...]
    y_ref[...] = (o * _silu(r_ref[...])).astype(y_ref.dtype)


def gla_gate(o_f, o_b, proj, norm_g, r_col0, row0, tm=512):
    nt, hv = o_f.shape
    dv = norm_g.shape[0]
    r_off = r_col0 // dv
    rb = row0 // tm
    assert row0 % tm == 0 and nt % tm == 0
    return pl.pallas_call(
        _gla_gate_body,
        grid=(nt // tm, hv // dv),
        in_specs=[pl.BlockSpec((tm, dv), lambda m, hh: (m, hh)),
                  pl.BlockSpec((tm, dv), lambda m, hh: (m, hh)),
                  pl.BlockSpec((tm, dv), lambda m, hh: (rb + m, r_off + hh)),
                  pl.BlockSpec((1, dv), lambda m, hh: (0, 0))],
        out_specs=pl.BlockSpec((tm, dv), lambda m, hh: (m, hh)),
        out_shape=jax.ShapeDtypeStruct((nt, hv), bf16),
        compiler_params=_cparams(2, 32),
        name="gla_gate",
    )(o_f, o_b, proj, norm_g.reshape(1, dv))


def _conv_body(x_ref, w_ref, b_ref, o_ref):
    x = x_ref[...]
    t = x.shape[0]
    row = lax.broadcasted_iota(jnp.int32, x.shape, 0)
    prev = jnp.where(row == 0, 0.0, pltpu.roll(x, 1, 0))
    nxt = jnp.where(row == t - 1, 0.0, pltpu.roll(x, t - 1, 0))
    w = w_ref[...]
    o_ref[...] = _silu(prev * w[0:1] + x * w[1:2] + nxt * w[2:3] + b_ref[...])


def ssd_conv(proj, conv_w, conv_b, col0, row0, n_seq, seq_len, tc=512):
    nc = conv_w.shape[1]
    assert col0 % tc == 0 and nc % tc == 0 and row0 % seq_len == 0
    cb, rb = col0 // tc, row0 // seq_len
    return pl.pallas_call(
        _conv_body,
        grid=(n_seq, nc // tc),
        in_specs=[pl.BlockSpec((seq_len, tc), lambda s, ci: (rb + s, cb + ci)),
                  pl.BlockSpec((3, tc), lambda s, ci: (0, ci)),
                  pl.BlockSpec((1, tc), lambda s, ci: (0, ci))],
        out_specs=pl.BlockSpec((seq_len, tc), lambda s, ci: (s, ci)),
        out_shape=jax.ShapeDtypeStruct((n_seq * seq_len, nc), f32),
        compiler_params=_cparams(2, 48),
        name="ssd_conv",
    )(proj, conv_w, conv_b.reshape(1, nc))


def _softplus(x):
    return jnp.maximum(x, 0.0) + jnp.log1p(jnp.exp(-jnp.abs(x)))


def _ssd_direction(x, bm, cm, dt_raw, bias, a_row, e_wide, e_head, ht_ref, reverse):
    tb, width = x.shape
    p = SSD_HEADDIM
    n_heads = width // p
    i = lax.broadcasted_iota(jnp.int32, (tb, tb), 0)
    j = lax.broadcasted_iota(jnp.int32, (tb, tb), 1)
    causal = (j >= i) if reverse else (j <= i)
    causal_bf = jnp.where(causal, 1.0, 0.0).astype(bf16)

    dt = _softplus(dt_raw + bias)
    cs = _dot_exact_lhs(causal_bf, dt * a_row)
    cs_w = _dot_exact_rhs(cs, e_wide)
    dt_w = _dot_exact_rhs(dt, e_wide)
    cs_h = _dot_exact_rhs(cs, e_head)
    cs_ht = cs_h.T
    last = 0 if reverse else tb - 1
    cs_last = cs_w[last:last + 1, :]

    bm_bf = bm.astype(bf16)
    cm_bf = cm.astype(bf16)
    cb = _dot_nt(cm_bf, bm_bf)
    xdt = (x * dt_w).astype(bf16)
    lane = lax.broadcasted_iota(jnp.int32, (tb, LANES), 1)
    heads_per_slab = LANES // p
    y_slabs = []
    for sidx in range(width // LANES):
        xs = xdt[:, sidx * LANES:(sidx + 1) * LANES]
        acc = None
        for hs in range(heads_per_slab):
            hh = sidx * heads_per_slab + hs
            diff = cs_h[:, hh:hh + 1] - cs_ht[hh:hh + 1, :]
            m = (cb * jnp.exp(jnp.where(causal, diff, -jnp.inf))).astype(bf16)
            part = _dot(m, jnp.where((lane // p) == hs, xs, jnp.zeros_like(xs)))
            acc = part if acc is None else acc + part
        y_slabs.append(acc)
    ht = ht_ref[...]
    y = jnp.concatenate(y_slabs, axis=1) + _dot(cm_bf, ht.astype(bf16)) * jnp.exp(cs_w)
    xw = (x * (jnp.exp(cs_last - cs_w) * dt_w)).astype(bf16)
    ht_ref[...] = ht * jnp.exp(cs_last) + _dot_tn(bm_bf, xw)
    return y


def _ssd_body(*refs, has_init, emit_state, nb):
    refs = list(refs)
    xf, bf_, cf, dtf, xb, bb, cb_, dtb, bias, alog, ewf, ewb, ehf, ehb = refs[:14]
    refs = refs[14:]
    if has_init:
        s0_ref = refs[0]
        refs = refs[1:]
    yf_ref, yb_ref = refs[:2]
    refs = refs[2:]
    if emit_state:
        sfin_ref = refs[0]
        refs = refs[1:]
    htf_ref, htb_ref = refs
    t = pl.program_id(2)
    n, width = htf_ref.shape

    @pl.when(t == 0)
    def _():
        if has_init:
            htf_ref[...] = s0_ref[0, 0].reshape(width, n).T
            htb_ref[...] = s0_ref[0, 1].reshape(width, n).T
        else:
            htf_ref[...] = jnp.zeros_like(htf_ref)
            htb_ref[...] = jnp.zeros_like(htb_ref)

    a_row = -jnp.exp(alog[...])
    yf_ref[...] = _ssd_direction(xf[...], bf_[...], cf[...], dtf[...], bias[...], a_row, ewf[...], ehf[...],
                                 htf_ref, False)
    yb_ref[...] = _ssd_direction(xb[...], bb[...], cb_[...], dtb[...], bias[...], a_row, ewb[...], ehb[...],
                                 htb_ref, True)

    if emit_state:
        @pl.when(t == nb - 1)
        def _():
            sfin_ref[0, 0] = htf_ref[...].T.reshape(sfin_ref.shape[2:])
            sfin_ref[0, 1] = htb_ref[...].T.reshape(sfin_ref.shape[2:])


def _ssd_select_matrices(n_heads):
    hg = n_heads // SSD_GROUPS
    col = jnp.arange(2 * n_heads)[None, None, :, None]
    d = jnp.arange(2)[:, None, None, None]
    g = jnp.arange(SSD_GROUPS)[None, :, None, None]
    wide = jnp.arange(hg * SSD_HEADDIM)[None, None, None, :] // SSD_HEADDIM
    head = jnp.arange(LANES)[None, None, None, :]
    first = d * n_heads + g * hg
    e_wide = (col == first + wide).astype(bf16)
    e_head = jnp.logical_and(col == first + head, head < hg).astype(bf16)
    return e_wide, e_head


def ssd_scan(xbc, dt, dt_bias, a_log, row0, n_seq, seq_len, s0=None, emit_state=False, tb=256):
    n_heads = dt_bias.shape[1]
    assert 2 * n_heads == LANES
    hg = n_heads // SSD_GROUPS
    p, n = SSD_HEADDIM, SSD_STATE
    width = hg * p
    di = n_heads * p
    nb = seq_len // tb
    base = row0 // tb
    assert seq_len % tb == 0 and row0 % tb == 0 and n == LANES
    e_wide, e_head = _ssd_select_matrices(n_heads)
    b_off = di // n
    c_off = b_off + SSD_GROUPS

    in_specs, args = [], []
    for rev in (False, True):
        def blk(s, t, rev=rev):
            return s * nb + ((nb - 1 - t) if rev else t)
        in_specs += [pl.BlockSpec((tb, width), lambda s, g, t, blk=blk: (blk(s, t), g)),
                     pl.BlockSpec((tb, n), lambda s, g, t, blk=blk: (blk(s, t), b_off + g)),
                     pl.BlockSpec((tb, n), lambda s, g, t, blk=blk: (blk(s, t), c_off + g)),
                     pl.BlockSpec((tb, LANES), lambda s, g, t, blk=blk: (base + blk(s, t), 0))]
        args += [xbc, xbc, xbc, dt]
    in_specs += [pl.BlockSpec((1, LANES), lambda s, g, t: (0, 0)),
                 pl.BlockSpec((1, LANES), lambda s, g, t: (0, 0)),
                 pl.BlockSpec((None, None, LANES, width), lambda s, g, t: (0, g, 0, 0)),
                 pl.BlockSpec((None, None, LANES, width), lambda s, g, t: (1, g, 0, 0)),
                 pl.BlockSpec((None, None, LANES, LANES), lambda s, g, t: (0, g, 0, 0)),
                 pl.BlockSpec((None, None, LANES, LANES), lambda s, g, t: (1, g, 0, 0))]
    args += [dt_bias.reshape(1, LANES), a_log.reshape(1, LANES), e_wide, e_wide, e_head, e_head]
    if s0 is not None:
        in_specs.append(pl.BlockSpec((1, 2, hg, p, n), lambda s, g, t: (s, 0, g, 0, 0)))
        args.append(s0)
    n_rows = n_seq * seq_len
    out_specs = [pl.BlockSpec((tb, width), lambda s, g, t: (s * nb + t, g)),
                 pl.BlockSpec((tb, width), lambda s, g, t: (s * nb + (nb - 1 - t), g))]
    out_shape = [jax.ShapeDtypeStruct((n_rows, di), f32)] * 2
    if emit_state:
        out_specs.append(pl.BlockSpec((1, 2, hg, p, n), lambda s, g, t: (s, 0, g, 0, 0)))
        out_shape.append(jax.ShapeDtypeStruct((n_seq, 2, n_heads, p, n), f32))
    return pl.pallas_call(
        functools.partial(_ssd_body, has_init=s0 is not None, emit_state=emit_state, nb=nb),
        grid=(n_seq, SSD_GROUPS, nb),
        in_specs=in_specs,
        out_specs=out_specs,
        out_shape=out_shape,
        scratch_shapes=[pltpu.VMEM((n, width), f32), pltpu.VMEM((n, width), f32)],
        compiler_params=_cparams(3, 40),
        name="ssd_scan",
    )(*args)


def _ssd_gate_body(yf_ref, yb_ref, xs_ref, z_ref, d_ref, g_ref, o_ref):
    y = yf_ref[...] + yb_ref[...] + d_ref[...] * xs_ref[...]
    y = y * _silu(z_ref[...])
    y = y * lax.rsqrt(jnp.mean(y * y, axis=-1, keepdims=True) + RMS_EPS)
    o_ref[...] = (y * g_ref[...]).astype(o_ref.dtype)


def ssd_gate(y_f, y_b, xbc, proj, d_rep, norm_g, row0, tm=512):
    nt, di = y_f.shape
    gw = di // SSD_GROUPS
    rb = row0 // tm
    assert row0 % tm == 0 and nt % tm == 0
    return pl.pallas_call(
        _ssd_gate_body,
        grid=(nt // tm, SSD_GROUPS),
        in_specs=[pl.BlockSpec((tm, gw), lambda m, gg: (m, gg)),
                  pl.BlockSpec((tm, gw), lambda m, gg: (m, gg)),
                  pl.BlockSpec((tm, gw), lambda m, gg: (m, gg)),
                  pl.BlockSpec((tm, gw), lambda m, gg: (rb + m, gg)),
                  pl.BlockSpec((1, gw), lambda m, gg: (0, gg)),
                  pl.BlockSpec((1, gw), lambda m, gg: (0, gg))],
        out_specs=pl.BlockSpec((tm, gw), lambda m, gg: (m, gg)),
        out_shape=jax.ShapeDtypeStruct((nt, di), bf16),
        compiler_params=_cparams(2, 32),
        name="ssd_gate",
    )(y_f, y_b, xbc, proj, d_rep.reshape(1, di), norm_g.reshape(1, di))


def _router_body(x_ref, sh_ref, sc_ref, w_ref, b_ref, h_ref, idx_ref, wt_ref, *, n_exp):
    h = x_ref[...] * (1.0 + sc_ref[0]) + sh_ref[0]
    h_ref[...] = h
    tm = h.shape[0]
    scores = jax.nn.sigmoid(_dot(h.astype(bf16), w_ref[...].astype(bf16)))
    choice = scores + b_ref[...]
    n_grp = N_EXPERT_GROUPS
    per_grp = n_exp // n_grp
    s3 = scores.T[:n_exp].reshape(n_grp, per_grp, tm)
    c3 = choice.T[:n_exp].reshape(n_grp, per_grp, tm)
    ninf = -jnp.inf
    pos = lax.broadcasted_iota(jnp.int32, c3.shape, 1)
    grp = lax.broadcasted_iota(jnp.int32, c3.shape, 0)
    eidx = grp * per_grp + pos

    m1 = jnp.max(c3, axis=1, keepdims=True)
    i1 = jnp.min(jnp.where(c3 == m1, pos, per_grp), axis=1, keepdims=True)
    m2 = jnp.max(jnp.where(pos == i1, ninf, c3), axis=1, keepdims=True)
    gscore = m1 + m2
    gpos = lax.broadcasted_iota(jnp.int32, gscore.shape, 0)
    keep = jnp.zeros(gscore.shape, jnp.bool_)
    for _ in range(TOPK_GROUPS):
        m = jnp.max(gscore, axis=0, keepdims=True)
        gi = jnp.min(jnp.where(gscore == m, gpos, n_grp), axis=0, keepdims=True)
        hit = gpos == gi
        keep = jnp.logical_or(keep, hit)
        gscore = jnp.where(hit, ninf, gscore)

    cur = jnp.where(keep, c3, ninf)
    picked_w = []
    for k in range(TOP_K):
        m = jnp.max(jnp.max(cur, axis=1, keepdims=True), axis=0, keepdims=True)
        ei = jnp.min(jnp.min(jnp.where(cur == m, eidx, n_exp), axis=1, keepdims=True), axis=0, keepdims=True)
        hit = eidx == ei
        wk = jnp.sum(jnp.sum(jnp.where(hit, s3, 0.0), axis=1, keepdims=True), axis=0, keepdims=True)
        cur = jnp.where(hit, ninf, cur)
        idx_ref[k:k + 1, :] = ei.reshape(1, tm)
        picked_w.append(wk.reshape(1, tm))
    total = picked_w[0]
    for wk in picked_w[1:]:
        total = total + wk
    for k in range(TOP_K):
        wt_ref[k:k + 1, :] = picked_w[k] / total * ROUTED_SCALE


def moe_router(x, mods, rows, layer, w_router, router_bias, tm=512):
    nt, d = x.shape
    n_exp = w_router.shape[1]
    assert n_exp <= LANES and n_exp % N_EXPERT_GROUPS == 0
    w_pad = jnp.pad(w_router, ((0, 0), (0, LANES - n_exp)))
    b_pad = jnp.pad(router_bias, (0, LANES - n_exp)).reshape(1, LANES)
    sh = _mod_index(rows, layer, 3, tm)
    sc = _mod_index(rows, layer, 4, tm)
    return pl.pallas_call(
        functools.partial(_router_body, n_exp=n_exp),
        grid=(nt // tm,),
        in_specs=[pl.BlockSpec((tm, d), lambda m: (m, 0)),
                  pl.BlockSpec((1, 1, d), sh),
                  pl.BlockSpec((1, 1, d), sc),
                  pl.BlockSpec((d, LANES), lambda m: (0, 0)),
                  pl.BlockSpec((1, LANES), lambda m: (0, 0))],
        out_specs=[pl.BlockSpec((tm, d), lambda m: (m, 0)),
                   pl.BlockSpec((TOP_K, tm), lambda m: (0, m)),
                   pl.BlockSpec((TOP_K, tm), lambda m: (0, m))],
        out_shape=[jax.ShapeDtypeStruct((nt, d), f32),
                   jax.ShapeDtypeStruct((TOP_K, nt), jnp.int32),
                   jax.ShapeDtypeStruct((TOP_K, nt), f32)],
        compiler_params=_cparams(1, 32),
        name="moe_router",
    )(x, mods, mods, w_pad, b_pad)


def routing_tables(idx_t, n_exp, tmx):
    nt = idx_t.shape[1]
    tk = TOP_K * nt
    e_flat = idx_t.reshape(tk)
    order = jnp.argsort(e_flat).astype(jnp.int32)
    e_sorted = e_flat[order]
    start = jnp.searchsorted(e_sorted, jnp.arange(n_exp, dtype=jnp.int32), side="left").astype(jnp.int32)
    counts = jnp.diff(jnp.concatenate([start, jnp.array([tk], jnp.int32)]))
    padded = (counts + tmx - 1) // tmx * tmx
    pend = jnp.cumsum(padded)
    pstart = pend - padded
    dest_sorted = pstart[e_sorted] + (jnp.arange(tk, dtype=jnp.int32) - start[e_sorted])
    n_blocks = -(-(tk + n_exp * (tmx - 1)) // tmx)
    tok_buf = jnp.zeros((n_blocks * tmx,), jnp.int32).at[dest_sorted].set(order % nt)
    dest = jnp.zeros((tk,), jnp.int32).at[order].set(dest_sorted)
    blk_start = jnp.arange(n_blocks, dtype=jnp.int32) * tmx
    blk_expert = jnp.minimum(jnp.searchsorted(pend, blk_start, side="right"), n_exp - 1).astype(jnp.int32)
    n_used = (pend[-1:] // tmx).astype(jnp.int32)
    return tok_buf.reshape(n_blocks, 1, tmx), dest, blk_expert, n_used


def _row_copy(src_hbm, row, dst_ref, dst_row, sem):
    return pltpu.make_async_copy(src_hbm.at[pl.ds(row, 1), :], dst_ref.at[pl.ds(dst_row, 1), :], sem)


def _expert_body(be_ref, nu_ref, tok_hbm, h_hbm, wg_ref, wu_ref, wd_ref, o_ref,
                 xbuf, idx_smem, wgb, wub, wdb, gsem, isem, *, tmx):
    i = pl.program_id(0)
    n_used = nu_ref[0]
    slot = i % 2

    def fetch_rows(blk, s):
        ids = pltpu.make_async_copy(tok_hbm.at[blk], idx_smem.at[s], isem)
        ids.start()
        ids.wait()

        def issue(r, c):
            _row_copy(h_hbm, idx_smem[s, 0, r], xbuf.at[s], r, gsem.at[s]).start()
            return c

        lax.fori_loop(0, tmx, issue, 0)

    @pl.when(jnp.logical_and(i == 0, n_used > 0))
    def _():
        fetch_rows(0, 0)

    @pl.when(i + 1 < n_used)
    def _():
        fetch_rows(i + 1, 1 - slot)

    first_of_expert = jnp.logical_or(i == 0, be_ref[i] != be_ref[jnp.maximum(i - 1, 0)])

    @pl.when(jnp.logical_and(i < n_used, first_of_expert))
    def _():
        _cast_rows(wg_ref, wgb, 256)
        _cast_rows(wu_ref, wub, 256)
        _cast_rows(wd_ref, wdb, 256)

    @pl.when(i < n_used)
    def _():
        def drain(r, c):
            _row_copy(h_hbm, 0, xbuf.at[slot], r, gsem.at[slot]).wait()
            return c

        lax.fori_loop(0, tmx, drain, 0)
        x = xbuf[slot].astype(bf16)
        hmid = (_silu(_dot(x, wgb[...])) * _dot(x, wub[...])).astype(bf16)
        o_ref[...] = _dot(hmid, wdb[...])

    @pl.when(i >= n_used)
    def _():
        o_ref[...] = jnp.zeros_like(o_ref)


def moe_experts(h, tok_buf, blk_expert, n_used, w_gate, w_up, w_down, layer, tmx):
    n_blocks = tok_buf.shape[0]
    d = h.shape[1]
    ff = w_gate.shape[3]
    grid_spec = pltpu.PrefetchScalarGridSpec(
        num_scalar_prefetch=2,
        grid=(n_blocks,),
        in_specs=[pl.BlockSpec(memory_space=pl.ANY),
                  pl.BlockSpec(memory_space=pl.ANY),
                  pl.BlockSpec((None, None, d, ff), lambda i, be, nu: (layer, be[i], 0, 0)),
                  pl.BlockSpec((None, None, d, ff), lambda i, be, nu: (layer, be[i], 0, 0)),
                  pl.BlockSpec((None, None, ff, d), lambda i, be, nu: (layer, be[i], 0, 0))],
        out_specs=pl.BlockSpec((tmx, d), lambda i, be, nu: (i, 0)),
        scratch_shapes=[pltpu.VMEM((2, tmx, d), f32),
                        pltpu.SMEM((2, 1, tmx), jnp.int32),
                        pltpu.VMEM((d, ff), bf16), pltpu.VMEM((d, ff), bf16), pltpu.VMEM((ff, d), bf16),
                        pltpu.SemaphoreType.DMA((2,)), pltpu.SemaphoreType.DMA(())],
    )
    return pl.pallas_call(
        functools.partial(_expert_body, tmx=tmx),
        grid_spec=grid_spec,
        out_shape=jax.ShapeDtypeStruct((n_blocks * tmx, d), f32),
        compiler_params=_cparams(1, 48),
        name="moe_experts",
    )(blk_expert, n_used, tok_buf, h, w_gate, w_up, w_down)


def _ffn_post_body(dest_hbm, rows_hbm, h_ref, wt_ref, sg_ref, su_ref, sd_ref, x_ref, gate_ref, g_ref, b_ref,
                   o_ref, rbuf, idx_smem, gsem, isem, *, alpha):
    i = pl.program_id(0)
    n = pl.num_programs(0)
    slot = i % 2
    tm = h_ref.shape[0]

    def fetch_rows(tile, s):
        ids = pltpu.make_async_copy(dest_hbm.at[tile], idx_smem.at[s], isem)
        ids.start()
        ids.wait()

        def issue(r, c):
            _row_copy(rows_hbm, idx_smem[s, 0, r], rbuf.at[s], r, gsem.at[s]).start()
            return c

        lax.fori_loop(0, TOP_K * tm, issue, 0)

    @pl.when(i == 0)
    def _():
        fetch_rows(0, 0)

    @pl.when(i + 1 < n)
    def _():
        fetch_rows(i + 1, 1 - slot)

    hb = h_ref[...].astype(bf16)
    hmid = (_silu(_dot(hb, sg_ref[...])) * _dot(hb, su_ref[...])).astype(bf16)
    ffn = _dot(hmid, sd_ref[...])

    def drain(r, c):
        _row_copy(rows_hbm, 0, rbuf.at[slot], r, gsem.at[slot]).wait()
        return c

    lax.fori_loop(0, TOP_K * tm, drain, 0)
    wt = wt_ref[...]
    routed = None
    for k in range(TOP_K):
        part = rbuf[slot, pl.ds(k * tm, tm), :] * wt[:, k:k + 1]
        routed = part if routed is None else routed + part
    v = alpha * x_ref[...] + gate_ref[0] * (routed + ffn)
    o_ref[...] = _layer_norm(v, g_ref[...], b_ref[...])


def moe_ffn_post(h, x, out_buf, dest, w_t, sg_bf, su_bf, sd_bf, mods, rows, layer, ln_g, ln_b, alpha, tm=128):
    nt, d = x.shape
    ff = sg_bf.shape[1]
    n_tiles = nt // tm
    dest_tiles = dest.reshape(TOP_K, n_tiles, tm).transpose(1, 0, 2).reshape(n_tiles, 1, TOP_K * tm)
    w_tk = w_t.T
    gate = _mod_index(rows, layer, 5, tm)
    return pl.pallas_call(
        functools.partial(_ffn_post_body, alpha=alpha),
        grid=(n_tiles,),
        in_specs=[pl.BlockSpec(memory_space=pl.ANY),
                  pl.BlockSpec(memory_space=pl.ANY),
                  pl.BlockSpec((tm, d), lambda m: (m, 0)),
                  pl.BlockSpec((tm, TOP_K), lambda m: (m, 0)),
                  pl.BlockSpec((d, ff), lambda m: (0, 0)),
                  pl.BlockSpec((d, ff), lambda m: (0, 0)),
                  pl.BlockSpec((ff, d), lambda m: (0, 0)),
                  pl.BlockSpec((tm, d), lambda m: (m, 0)),
                  pl.BlockSpec((1, 1, d), gate),
                  pl.BlockSpec((1, d), lambda m: (0, 0)),
                  pl.BlockSpec((1, d), lambda m: (0, 0))],
        out_specs=pl.BlockSpec((tm, d), lambda m: (m, 0)),
        out_shape=jax.ShapeDtypeStruct((nt, d), f32),
        scratch_shapes=[pltpu.VMEM((2, TOP_K * tm, d), f32),
                        pltpu.SMEM((2, 1, TOP_K * tm), jnp.int32),
                        pltpu.SemaphoreType.DMA((2,)), pltpu.SemaphoreType.DMA(())],
        compiler_params=_cparams(1, 48),
        name="moe_ffn_post",
    )(dest_tiles, out_buf, h, w_tk, sg_bf, su_bf, sd_bf, x, mods, ln_g.reshape(1, d), ln_b.reshape(1, d))


MOE_ROWS_PER_BLOCK = 256


def _grid_rope(seq_len, dk):
    t = jnp.arange(seq_len)
    row = (t // GRID_W).astype(f32)
    col = (t % GRID_W).astype(f32)
    nf = dk // 4
    freqs = ROPE_THETA ** (-jnp.arange(nf, dtype=f32) / nf)
    ang = jnp.concatenate([row[:, None] * freqs, col[:, None] * freqs], axis=-1)
    return jnp.cos(ang), jnp.sin(ang)


def kernel(x_prompt, x_sample, c, state_gla, state_ssd, c_ctx, w_ada, b_ada, ln_g, ln_b,
           gla_w_in, gla_w_gate, gla_b_gate, gla_norm_g, gla_w_out,
           ssd_w_in, ssd_conv_w, ssd_conv_b, ssd_dt_bias, ssd_a_log, ssd_d, ssd_norm_g, ssd_w_out,
           moe_w_router, moe_router_bias, moe_w_gate, moe_w_up, moe_w_down,
           shared_w_gate, shared_w_up, shared_w_down):
    bp, tp, d = x_prompt.shape
    bs, ts, _ = x_sample.shape
    depth = w_ada.shape[0]
    n_prompt = bp * tp
    rows = Rows(n_prompt, ts, bs)
    alpha = (2 * depth) ** 0.25
    x = jnp.concatenate([x_prompt.reshape(n_prompt, d), x_sample.reshape(bs * ts, d)], axis=0)

    cond = jnp.zeros((8, d), f32).at[0].set(c_ctx).at[1:1 + bs].set(c)
    mods = ada_all(cond, w_ada, b_ada)[:, :1 + bs].reshape(depth * rows.mods_per_layer, 1, d)

    hk = gla_w_gate.shape[-1]
    hv = gla_w_out.shape[1]
    rope = _grid_rope(ts, hk // GLA_HEADS)
    di = ssd_w_out.shape[1]
    n_ssd_heads = ssd_dt_bias.shape[-1]
    conv_dim = ssd_conv_w.shape[-1]
    n_exp = moe_w_router.shape[-1]

    new_gla, new_ssd = [], []
    for l in range(depth):
        j = l // 2
        if l % 2 == 0:
            w_in = gla_w_in[j]
            n_main = 2 * hk + 2 * hv
            proj = mm_in(x, mods, rows, l, w_in, 0, n_main, 1024)
            w_lowrank = jnp.pad(w_in[:, n_main:], ((0, 0), (0, LANES - 2 * GLA_RANK)))
            g = mm_in(x, mods, rows, l, w_lowrank, 0, LANES, LANES)
            wg_pad = (jnp.zeros((2, LANES, hk), f32)
                      .at[0, :GLA_RANK].set(gla_w_gate[j, 0])
                      .at[1, GLA_RANK:2 * GLA_RANK].set(gla_w_gate[j, 1]))
            of_p, ob_p, st = gla_scan(proj, g, wg_pad, gla_b_gate[j], 0, bp, tp, None, None, True)
            of_s, ob_s = gla_scan(proj, g, wg_pad, gla_b_gate[j], n_prompt, bs, ts, rope, state_gla[:, j], False)
            y = jnp.concatenate([gla_gate(of_p, ob_p, proj, gla_norm_g[j], 2 * hk + hv, 0),
                                 gla_gate(of_s, ob_s, proj, gla_norm_g[j], 2 * hk + hv, n_prompt)], axis=0)
            w_out = gla_w_out[j].astype(bf16)
            new_gla.append(st)
        else:
            w_in = ssd_w_in[j]
            n_main = di + conv_dim
            proj = mm_in(x, mods, rows, l, w_in, 0, n_main, 1024)
            dt = mm_in(x, mods, rows, l, w_in, n_main, 2 * n_ssd_heads, 2 * n_ssd_heads)
            d_rep = jnp.repeat(ssd_d[j], SSD_HEADDIM)
            xbc_p = ssd_conv(proj, ssd_conv_w[j], ssd_conv_b[j], di, 0, bp, tp)
            xbc_s = ssd_conv(proj, ssd_conv_w[j], ssd_conv_b[j], di, n_prompt, bs, ts)
            yf_p, yb_p, st = ssd_scan(xbc_p, dt, ssd_dt_bias[j], ssd_a_log[j], 0, bp, tp, None, True)
            yf_s, yb_s = ssd_scan(xbc_s, dt, ssd_dt_bias[j], ssd_a_log[j], n_prompt, bs, ts, state_ssd[:, j], False)
            y = jnp.concatenate([ssd_gate(yf_p, yb_p, xbc_p, proj, d_rep, ssd_norm_g[j], 0),
                                 ssd_gate(yf_s, yb_s, xbc_s, proj, d_rep, ssd_norm_g[j], n_prompt)], axis=0)
            w_out = ssd_w_out[j].astype(bf16)
            new_ssd.append(st)
        x1 = mm_post(y, w_out, x, mods, rows, l, 2, ln_g[l, 0], ln_b[l, 0], alpha)

        h, idx_t, w_t = moe_router(x1, mods, rows, l, moe_w_router[l], moe_router_bias[l])
        tok_buf, dest, blk_expert, n_used = routing_tables(idx_t, n_exp, MOE_ROWS_PER_BLOCK)
        out_buf = moe_experts(h, tok_buf, blk_expert, n_used, moe_w_gate, moe_w_up, moe_w_down, l, MOE_ROWS_PER_BLOCK)
        x = moe_ffn_post(h, x1, out_buf, dest, w_t, shared_w_gate[l].astype(bf16), shared_w_up[l].astype(bf16),
                         shared_w_down[l].astype(bf16), mods, rows, l, ln_g[l, 1], ln_b[l, 1], alpha)

    y_prompt = x[:n_prompt].reshape(bp, tp, d)
    y_sample = x[n_prompt:].reshape(bs, ts, d)
    return y_prompt, y_sample, jnp.stack(new_gla, axis=1), jnp.stack(new_ssd, axis=1)
```

```python
import functools
import math

import jax
import jax.numpy as jnp
from jax import lax
from jax.experimental import pallas as pl
from jax.experimental.pallas import tpu as pltpu

f32 = jnp.float32
bf16 = jnp.bfloat16

GRID_W = 64
GLA_HEADS = 4
GLA_RANK = 16
GLA_TAU = 16.0
GLA_CHUNK = 32
ROPE_THETA = 10000.0
SSD_HEADDIM = 64
SSD_STATE = 128
SSD_GROUPS = 8
N_EXPERT_GROUPS = 8
TOPK_GROUPS = 4
TOP_K = 8
ROUTED_SCALE = 2.5
LN_EPS = 1e-5
RMS_EPS = 1e-6

LANES = 128
VMEM_BYTES_V7X = 64 * 2 ** 20


def _cparams(n_axes, vmem_mib):
    assert vmem_mib * 2 ** 20 < VMEM_BYTES_V7X
    return pltpu.CompilerParams(dimension_semantics=("arbitrary",) * n_axes,
                                vmem_limit_bytes=vmem_mib * 2 ** 20)


def _dot(a, b):
    return jnp.dot(a, b, preferred_element_type=f32)


def _dot_nt(a, b):
    return lax.dot_general(a, b, (((1,), (1,)), ((), ())), preferred_element_type=f32)


def _dot_tn(a, b):
    return lax.dot_general(a, b, (((0,), (0,)), ((), ())), preferred_element_type=f32)


def _split3(x):
    hi = x.astype(bf16)
    r = x - hi.astype(f32)
    mid = r.astype(bf16)
    lo = (r - mid.astype(f32)).astype(bf16)
    return hi, mid, lo


def _dot_exact_lhs(a_bf, x):
    hi, mid, lo = _split3(x)
    return (_dot(a_bf, lo) + _dot(a_bf, mid)) + _dot(a_bf, hi)


def _dot_exact_rhs(x, e_bf):
    hi, mid, lo = _split3(x)
    return (_dot(lo, e_bf) + _dot(mid, e_bf)) + _dot(hi, e_bf)


def _silu(x):
    return x * jax.nn.sigmoid(x)


def _cast_rows(src_ref, dst_ref, rows):
    n = src_ref.shape[0]
    rows = math.gcd(n, rows)

    def body(i, c):
        r = pl.multiple_of(i * rows, rows)
        dst_ref[pl.ds(r, rows), :] = src_ref[pl.ds(r, rows), :].astype(dst_ref.dtype)
        return c

    lax.fori_loop(0, n // rows, body, 0)


def _ada_body(c_ref, w_ref, b_ref, o_ref):
    a = _silu(c_ref[...]).astype(bf16)
    o_ref[0] = _dot(a, w_ref[0].astype(bf16)) + b_ref[0]


def ada_all(cond8, w_ada, b_ada, tn=1024):
    L, D, N = w_ada.shape
    return pl.pallas_call(
        _ada_body,
        grid=(L, N // tn),
        in_specs=[pl.BlockSpec((8, D), lambda l, n: (0, 0)),
                  pl.BlockSpec((1, D, tn), lambda l, n: (l, 0, n)),
                  pl.BlockSpec((1, 1, tn), lambda l, n: (l, 0, n))],
        out_specs=pl.BlockSpec((1, 8, tn), lambda l, n: (l, 0, n)),
        out_shape=jax.ShapeDtypeStruct((L, 8, N), f32),
        compiler_params=_cparams(2, 40),
        name="ada",
    )(cond8, w_ada, b_ada.reshape(L, 1, N))


N_ADA = 6


class Rows:
    def __init__(self, n_prompt, seq_s, n_seq_s):
        self.n_prompt = n_prompt
        self.seq_s = seq_s
        self.mods_per_layer = (1 + n_seq_s) * N_ADA

    def group(self, row0):
        return jnp.where(row0 < self.n_prompt, 0, 1 + (row0 - self.n_prompt) // self.seq_s)


def _mod_index(rows, layer, j, tm):
    assert rows.n_prompt % tm == 0 and rows.seq_s % tm == 0

    def index_map(i):
        return (layer * rows.mods_per_layer + rows.group(i * tm) * N_ADA + j, 0, 0)
    return index_map


def _mm_in_body(x_ref, sh_ref, sc_ref, w_ref, o_ref, wb_ref):
    @pl.when(pl.program_id(1) == 0)
    def _():
        _cast_rows(w_ref, wb_ref, 256)

    h = (x_ref[...] * (1.0 + sc_ref[0]) + sh_ref[0]).astype(bf16)
    o_ref[...] = _dot(h, wb_ref[...])


def mm_in(x, mods, rows, layer, w, col0, n_cols, tn, tm=512):
    nt, d = x.shape
    assert n_cols % tn == 0 and col0 % tn == 0 and nt % tm == 0
    sh = _mod_index(rows, layer, 0, tm)
    sc = _mod_index(rows, layer, 1, tm)
    cb = col0 // tn
    return pl.pallas_call(
        _mm_in_body,
        grid=(n_cols // tn, nt // tm),
        in_specs=[pl.BlockSpec((tm, d), lambda n, m: (m, 0)),
                  pl.BlockSpec((1, 1, d), lambda n, m: sh(m)),
                  pl.BlockSpec((1, 1, d), lambda n, m: sc(m)),
                  pl.BlockSpec((d, tn), lambda n, m: (0, cb + n))],
        out_specs=pl.BlockSpec((tm, tn), lambda n, m: (m, n)),
        out_shape=jax.ShapeDtypeStruct((nt, n_cols), f32),
        scratch_shapes=[pltpu.VMEM((d, tn), bf16)],
        compiler_params=_cparams(2, 48),
        name="mm_in",
    )(x, mods, mods, w)


def _layer_norm(v, g, b):
    mu = jnp.mean(v, axis=-1, keepdims=True)
    d = v - mu
    var = jnp.mean(d * d, axis=-1, keepdims=True)
    return d * lax.rsqrt(var + LN_EPS) * g + b


def _mm_post_body(y_ref, w_ref, x_ref, gate_ref, g_ref, b_ref, o_ref, acc_ref, *, nk, alpha):
    k = pl.program_id(1)
    part = _dot(y_ref[...], w_ref[...])

    @pl.when(k == 0)
    def _():
        acc_ref[...] = part

    @pl.when(k > 0)
    def _():
        acc_ref[...] += part

    @pl.when(k == nk - 1)
    def _():
        v = alpha * x_ref[...] + gate_ref[0] * acc_ref[...]
        o_ref[...] = _layer_norm(v, g_ref[...], b_ref[...])


def mm_post(y, w_bf, x, mods, rows, layer, gate_j, ln_g, ln_b, alpha, tm=256, tk=2048):
    nt, kdim = y.shape
    d = x.shape[1]
    nk = kdim // tk
    gate = _mod_index(rows, layer, gate_j, tm)
    return pl.pallas_call(
        functools.partial(_mm_post_body, nk=nk, alpha=alpha),
        grid=(nt // tm, nk),
        in_specs=[pl.BlockSpec((tm, tk), lambda m, k: (m, k)),
                  pl.BlockSpec((tk, d), lambda m, k: (k, 0)),
                  pl.BlockSpec((tm, d), lambda m, k: (m, 0)),
                  pl.BlockSpec((1, 1, d), lambda m, k: gate(m)),
                  pl.BlockSpec((1, d), lambda m, k: (0, 0)),
                  pl.BlockSpec((1, d), lambda m, k: (0, 0))],
        out_specs=pl.BlockSpec((tm, d), lambda m, k: (m, 0)),
        out_shape=jax.ShapeDtypeStruct((nt, d), f32),
        scratch_shapes=[pltpu.VMEM((tm, d), f32)],
        compiler_params=_cparams(2, 48),
        name="mm_post",
    )(y, w_bf, x, mods, ln_g.reshape(1, d), ln_b.reshape(1, d))


def _chunk_masks(tb, c, reverse):
    i = lax.broadcasted_iota(jnp.int32, (tb, tb), 0)
    j = lax.broadcasted_iota(jnp.int32, (tb, tb), 1)
    same = (i // c) == (j // c)
    tri = (j >= i) if reverse else (j <= i)
    return jnp.logical_and(same, tri), same


def _gla_direction(q, k, v, g, wg, bg, rope, st_ref, reverse, c):
    tb, dk = q.shape
    n_chunks = tb // c
    causal, same = _chunk_masks(tb, c, reverse)
    causal_bf = jnp.where(causal, 1.0, 0.0).astype(bf16)
    same_bf = jnp.where(same, 1.0, 0.0).astype(bf16)

    pre = _dot(g.astype(bf16), wg.astype(bf16)) + bg
    log_a = (jnp.minimum(pre, 0.0) - jnp.log1p(jnp.exp(-jnp.abs(pre)))) * (1.0 / GLA_TAU)
    b = _dot_exact_lhs(causal_bf, log_a)
    total = _dot_exact_lhs(same_bf, log_a)

    q = q * (dk ** -0.5)
    if rope is not None:
        cos, sin = rope
        half = dk // 2
        q = jnp.concatenate([q[:, :half] * cos - q[:, half:] * sin, q[:, :half] * sin + q[:, half:] * cos], axis=1)
        k = jnp.concatenate([k[:, :half] * cos - k[:, half:] * sin, k[:, :half] * sin + k[:, half:] * cos], axis=1)

    q_dec = (q * jnp.exp(b)).astype(bf16)
    k_inv = (k * jnp.exp(-b)).astype(bf16)
    k_end = (k * jnp.exp(total - b)).astype(bf16)
    dec = jnp.exp(total)
    v_bf = v.astype(bf16)

    scores = jnp.where(causal, _dot_nt(q_dec, k_inv), 0.0)
    o_intra = _dot(scores.astype(bf16), v_bf)

    order = range(n_chunks - 1, -1, -1) if reverse else range(n_chunks)
    o_inter = [None] * n_chunks
    st = st_ref[...]
    for ci in order:
        sl = slice(ci * c, (ci + 1) * c)
        o_inter[ci] = _dot_nt(q_dec[sl], st.astype(bf16))
        st = st * dec[ci * c:ci * c + 1, :] + _dot_tn(v_bf[sl], k_end[sl])
    st_ref[...] = st
    return o_intra + jnp.concatenate(o_inter, axis=0)


def _gla_body(*refs, c, use_rope, has_init, emit_state, nb):
    refs = list(refs)
    qf, kf, vf, gf, qb, kb, vb, gb, wgf, wgb, bgf, bgb = refs[:12]
    refs = refs[12:]
    rope_f = rope_b = None
    if use_rope:
        rope_f = (refs[0][...], refs[1][...])
        rope_b = (refs[2][...], refs[3][...])
        refs = refs[4:]
    if has_init:
        s0_ref = refs[0]
        refs = refs[1:]
    of_ref, ob_ref = refs[:2]
    refs = refs[2:]
    if emit_state:
        sfin_ref = refs[0]
        refs = refs[1:]
    stf_ref, stb_ref = refs
    t = pl.program_id(2)

    @pl.when(t == 0)
    def _():
        if has_init:
            stf_ref[...] = s0_ref[0, 0, 0].T
            stb_ref[...] = s0_ref[0, 1, 0].T
        else:
            stf_ref[...] = jnp.zeros_like(stf_ref)
            stb_ref[...] = jnp.zeros_like(stb_ref)

    of_ref[...] = _gla_direction(qf[...], kf[...], vf[...], gf[...], wgf[...], bgf[0], rope_f, stf_ref, False, c)
    ob_ref[...] = _gla_direction(qb[...], kb[...], vb[...], gb[...], wgb[...], bgb[0], rope_b, stb_ref, True, c)

    if emit_state:
        @pl.when(t == nb - 1)
        def _():
            sfin_ref[0, 0, 0] = stf_ref[...].T
            sfin_ref[0, 1, 0] = stb_ref[...].T


def gla_scan(proj, g, wg_pad, b_gate, row0, n_seq, seq_len, rope=None, s0=None, emit_state=False, tb=256):
    h = GLA_HEADS
    hk = wg_pad.shape[2]
    dk = hk // h
    dv = 2 * dk
    nb = seq_len // tb
    base = row0 // tb
    assert seq_len % tb == 0 and row0 % tb == 0 and dv == 2 * dk

    def fblk(s, t):
        return base + s * nb + t

    def bblk(s, t):
        return base + s * nb + (nb - 1 - t)

    v_off = 2 * hk // dv
    in_specs, args = [], []
    for blk in (fblk, bblk):
        in_specs += [pl.BlockSpec((tb, dk), lambda s, hh, t, blk=blk: (blk(s, t), hh)),
                     pl.BlockSpec((tb, dk), lambda s, hh, t, blk=blk: (blk(s, t), h + hh)),
                     pl.BlockSpec((tb, dv), lambda s, hh, t, blk=blk: (blk(s, t), v_off + hh)),
                     pl.BlockSpec((tb, LANES), lambda s, hh, t, blk=blk: (blk(s, t), 0))]
        args += [proj, proj, proj, g]
    in_specs += [pl.BlockSpec((None, LANES, dk), lambda s, hh, t: (0, 0, hh)),
                 pl.BlockSpec((None, LANES, dk), lambda s, hh, t: (1, 0, hh)),
                 pl.BlockSpec((None, 1, dk), lambda s, hh, t: (0, 0, hh)),
                 pl.BlockSpec((None, 1, dk), lambda s, hh, t: (1, 0, hh))]
    bg3 = b_gate.reshape(2, 1, hk)
    args += [wg_pad, wg_pad, bg3, bg3]
    if rope is not None:
        for tab_blk in (lambda s, hh, t: (t, 0), lambda s, hh, t: (nb - 1 - t, 0)):
            in_specs += [pl.BlockSpec((tb, dk // 2), tab_blk), pl.BlockSpec((tb, dk // 2), tab_blk)]
            args += [rope[0], rope[1]]
    if s0 is not None:
        in_specs.append(pl.BlockSpec((1, 2, 1, dk, dv), lambda s, hh, t: (s, 0, hh, 0, 0)))
        args.append(s0)
    n_rows = n_seq * seq_len
    out_specs = [pl.BlockSpec((tb, dv), lambda s, hh, t: (s * nb + t, hh)),
                 pl.BlockSpec((tb, dv), lambda s, hh, t: (s * nb + (nb - 1 - t), hh))]
    out_shape = [jax.ShapeDtypeStruct((n_rows, h * dv), f32)] * 2
    if emit_state:
        out_specs.append(pl.BlockSpec((1, 2, 1, dk, dv), lambda s, hh, t: (s, 0, hh, 0, 0)))
        out_shape.append(jax.ShapeDtypeStruct((n_seq, 2, h, dk, dv), f32))
    return pl.pallas_call(
        functools.partial(_gla_body, c=GLA_CHUNK, use_rope=rope is not None, has_init=s0 is not None,
                          emit_state=emit_state, nb=nb),
        grid=(n_seq, h, nb),
        in_specs=in_specs,
        out_specs=out_specs,
        out_shape=out_shape,
        scratch_shapes=[pltpu.VMEM((dv, dk), f32), pltpu.VMEM((dv, dk), f32)],
        compiler_params=_cparams(3, 40),
        name="gla_scan",
    )(*args)


def _gla_gate_body(of_ref, ob_ref, r_ref, g_ref, y_ref):
    o = of_ref[...] + ob_ref[...]
    o = o * lax.rsqrt(jnp.mean(o * o, axis=-1, keepdims=True) + RMS_EPS) * g_ref[...]
    y_ref[...] = (o * _silu(r_ref[...])).astype(y_ref.dtype)


def gla_gate(o_f, o_b, proj, norm_g, r_col0, row0, tm=512):
    nt, hv = o_f.shape
    dv = norm_g.shape[0]
    r_off = r_col0 // dv
    rb = row0 // tm
    assert row0 % tm == 0 and nt % tm == 0
    return pl.pallas_call(
        _gla_gate_body,
        grid=(nt // tm, hv // dv),
        in_specs=[pl.BlockSpec((tm, dv), lambda m, hh: (m, hh)),
                  pl.BlockSpec((tm, dv), lambda m, hh: (m, hh)),
                  pl.BlockSpec((tm, dv), lambda m, hh: (rb + m, r_off + hh)),
                  pl.BlockSpec((1, dv), lambda m, hh: (0, 0))],
        out_specs=pl.BlockSpec((tm, dv), lambda m, hh: (m, hh)),
        out_shape=jax.ShapeDtypeStruct((nt, hv), bf16),
        compiler_params=_cparams(2, 32),
        name="gla_gate",
    )(o_f, o_b, proj, norm_g.reshape(1, dv))


def _conv_body(x_ref, w_ref, b_ref, o_ref):
    x = x_ref[...]
    t = x.shape[0]
    row = lax.broadcasted_iota(jnp.int32, x.shape, 0)
    prev = jnp.where(row == 0, 0.0, pltpu.roll(x, 1, 0))
    nxt = jnp.where(row == t - 1, 0.0, pltpu.roll(x, t - 1, 0))
    w = w_ref[...]
    o_ref[...] = _silu(prev * w[0:1] + x * w[1:2] + nxt * w[2:3] + b_ref[...])


def ssd_conv(proj, conv_w, conv_b, col0, row0, n_seq, seq_len, tc=512):
    nc = conv_w.shape[1]
    assert col0 % tc == 0 and nc % tc == 0 and row0 % seq_len == 0
    cb, rb = col0 // tc, row0 // seq_len
    return pl.pallas_call(
        _conv_body,
        grid=(n_seq, nc // tc),
        in_specs=[pl.BlockSpec((seq_len, tc), lambda s, ci: (rb + s, cb + ci)),
                  pl.BlockSpec((3, tc), lambda s, ci: (0, ci)),
                  pl.BlockSpec((1, tc), lambda s, ci: (0, ci))],
        out_specs=pl.BlockSpec((seq_len, tc), lambda s, ci: (s, ci)),
        out_shape=jax.ShapeDtypeStruct((n_seq * seq_len, nc), f32),
        compiler_params=_cparams(2, 48),
        name="ssd_conv",
    )(proj, conv_w, conv_b.reshape(1, nc))


def _softplus(x):
    return jnp.maximum(x, 0.0) + jnp.log1p(jnp.exp(-jnp.abs(x)))


def _ssd_direction(x, bm, cm, dt_raw, bias, a_row, e_wide, e_head, ht_ref, reverse):
    tb, width = x.shape
    p = SSD_HEADDIM
    n_heads = width // p
    i = lax.broadcasted_iota(jnp.int32, (tb, tb), 0)
    j = lax.broadcasted_iota(jnp.int32, (tb, tb), 1)
    causal = (j >= i) if reverse else (j <= i)
    causal_bf = jnp.where(causal, 1.0, 0.0).astype(bf16)

    dt = _softplus(dt_raw + bias)
    cs = _dot_exact_lhs(causal_bf, dt * a_row)
    cs_w = _dot_exact_rhs(cs, e_wide)
    dt_w = _dot_exact_rhs(dt, e_wide)
    cs_h = _dot_exact_rhs(cs, e_head)
    cs_ht = cs_h.T
    last = 0 if reverse else tb - 1
    cs_last = cs_w[last:last + 1, :]

    bm_bf = bm.astype(bf16)
    cm_bf = cm.astype(bf16)
    cb = _dot_nt(cm_bf, bm_bf)
    xdt = (x * dt_w).astype(bf16)
    lane = lax.broadcasted_iota(jnp.int32, (tb, LANES), 1)
    heads_per_slab = LANES // p
    y_slabs = []
    for sidx in range(width // LANES):
        xs = xdt[:, sidx * LANES:(sidx + 1) * LANES]
        acc = None
        for hs in range(heads_per_slab):
            hh = sidx * heads_per_slab + hs
            diff = cs_h[:, hh:hh + 1] - cs_ht[hh:hh + 1, :]
            m = (cb * jnp.exp(jnp.where(causal, diff, -jnp.inf))).astype(bf16)
            part = _dot(m, jnp.where((lane // p) == hs, xs, jnp.zeros_like(xs)))
            acc = part if acc is None else acc + part
        y_slabs.append(acc)
    ht = ht_ref[...]
    y = jnp.concatenate(y_slabs, axis=1) + _dot(cm_bf, ht.astype(bf16)) * jnp.exp(cs_w)
    xw = (x * (jnp.exp(cs_last - cs_w) * dt_w)).astype(bf16)
    ht_ref[...] = ht * jnp.exp(cs_last) + _dot_tn(bm_bf, xw)
    return y


def _ssd_body(*refs, has_init, emit_state, nb):
    refs = list(refs)
    xf, bf_, cf, dtf, xb, bb, cb_, dtb, bias, alog, ewf, ewb, ehf, ehb = refs[:14]
    refs = refs[14:]
    if has_init:
        s0_ref = refs[0]
        refs = refs[1:]
    yf_ref, yb_ref = refs[:2]
    refs = refs[2:]
    if emit_state:
        sfin_ref = refs[0]
        refs = refs[1:]
    htf_ref, htb_ref = refs
    t = pl.program_id(2)
    n, width = htf_ref.shape

    @pl.when(t == 0)
    def _():
        if has_init:
            htf_ref[...] = s0_ref[0, 0].reshape(width, n).T
            htb_ref[...] = s0_ref[0, 1].reshape(width, n).T
        else:
            htf_ref[...] = jnp.zeros_like(htf_ref)
            htb_ref[...] = jnp.zeros_like(htb_ref)

    a_row = -jnp.exp(alog[...])
    yf_ref[...] = _ssd_direction(xf[...], bf_[...], cf[...], dtf[...], bias[...], a_row, ewf[...], ehf[...],
                                 htf_ref, False)
    yb_ref[...] = _ssd_direction(xb[...], bb[...], cb_[...], dtb[...], bias[...], a_row, ewb[...], ehb[...],
                                 htb_ref, True)

    if emit_state:
        @pl.when(t == nb - 1)
        def _():
            sfin_ref[0, 0] = htf_ref[...].T.reshape(sfin_ref.shape[2:])
            sfin_ref[0, 1] = htb_ref[...].T.reshape(sfin_ref.shape[2:])


def _ssd_select_matrices(n_heads):
    hg = n_heads // SSD_GROUPS
    col = jnp.arange(2 * n_heads)[None, None, :, None]
    d = jnp.arange(2)[:, None, None, None]
    g = jnp.arange(SSD_GROUPS)[None, :, None, None]
    wide = jnp.arange(hg * SSD_HEADDIM)[None, None, None, :] // SSD_HEADDIM
    head = jnp.arange(LANES)[None, None, None, :]
    first = d * n_heads + g * hg
    e_wide = (col == first + wide).astype(bf16)
    e_head = jnp.logical_and(col == first + head, head < hg).astype(bf16)
    return e_wide, e_head


def ssd_scan(xbc, dt, dt_bias, a_log, row0, n_seq, seq_len, s0=None, emit_state=False, tb=256):
    n_heads = dt_bias.shape[1]
    assert 2 * n_heads == LANES
    hg = n_heads // SSD_GROUPS
    p, n = SSD_HEADDIM, SSD_STATE
    width = hg * p
    di = n_heads * p
    nb = seq_len // tb
    base = row0 // tb
    assert seq_len % tb == 0 and row0 % tb == 0 and n == LANES
    e_wide, e_head = _ssd_select_matrices(n_heads)
    b_off = di // n
    c_off = b_off + SSD_GROUPS

    in_specs, args = [], []
    for rev in (False, True):
        def blk(s, t, rev=rev):
            return s * nb + ((nb - 1 - t) if rev else t)
        in_specs += [pl.BlockSpec((tb, width), lambda s, g, t, blk=blk: (blk(s, t), g)),
                     pl.BlockSpec((tb, n), lambda s, g, t, blk=blk: (blk(s, t), b_off + g)),
                     pl.BlockSpec((tb, n), lambda s, g, t, blk=blk: (blk(s, t), c_off + g)),
                     pl.BlockSpec((tb, LANES), lambda s, g, t, blk=blk: (base + blk(s, t), 0))]
        args += [xbc, xbc, xbc, dt]
    in_specs += [pl.BlockSpec((1, LANES), lambda s, g, t: (0, 0)),
                 pl.BlockSpec((1, LANES), lambda s, g, t: (0, 0)),
                 pl.BlockSpec((None, None, LANES, width), lambda s, g, t: (0, g, 0, 0)),
                 pl.BlockSpec((None, None, LANES, width), lambda s, g, t: (1, g, 0, 0)),
                 pl.BlockSpec((None, None, LANES, LANES), lambda s, g, t: (0, g, 0, 0)),
                 pl.BlockSpec((None, None, LANES, LANES), lambda s, g, t: (1, g, 0, 0))]
    args += [dt_bias.reshape(1, LANES), a_log.reshape(1, LANES), e_wide, e_wide, e_head, e_head]
    if s0 is not None:
        in_specs.append(pl.BlockSpec((1, 2, hg, p, n), lambda s, g, t: (s, 0, g, 0, 0)))
        args.append(s0)
    n_rows = n_seq * seq_len
    out_specs = [pl.BlockSpec((tb, width), lambda s, g, t: (s * nb + t, g)),
                 pl.BlockSpec((tb, width), lambda s, g, t: (s * nb + (nb - 1 - t), g))]
    out_shape = [jax.ShapeDtypeStruct((n_rows, di), f32)] * 2
    if emit_state:
        out_specs.append(pl.BlockSpec((1, 2, hg, p, n), lambda s, g, t: (s, 0, g, 0, 0)))
        out_shape.append(jax.ShapeDtypeStruct((n_seq, 2, n_heads, p, n), f32))
    return pl.pallas_call(
        functools.partial(_ssd_body, has_init=s0 is not None, emit_state=emit_state, nb=nb),
        grid=(n_seq, SSD_GROUPS, nb),
        in_specs=in_specs,
        out_specs=out_specs,
        out_shape=out_shape,
        scratch_shapes=[pltpu.VMEM((n, width), f32), pltpu.VMEM((n, width), f32)],
        compiler_params=_cparams(3, 40),
        name="ssd_scan",
    )(*args)


def _ssd_gate_body(yf_ref, yb_ref, xs_ref, z_ref, d_ref, g_ref, o_ref):
    y = yf_ref[...] + yb_ref[...] + d_ref[...] * xs_ref[...]
    y = y * _silu(z_ref[...])
    y = y * lax.rsqrt(jnp.mean(y * y, axis=-1, keepdims=True) + RMS_EPS)
    o_ref[...] = (y * g_ref[...]).astype(o_ref.dtype)


def ssd_gate(y_f, y_b, xbc, proj, d_rep, norm_g, row0, tm=512):
    nt, di = y_f.shape
    gw = di // SSD_GROUPS
    rb = row0 // tm
    assert row0 % tm == 0 and nt % tm == 0
    return pl.pallas_call(
        _ssd_gate_body,
        grid=(nt // tm, SSD_GROUPS),
        in_specs=[pl.BlockSpec((tm, gw), lambda m, gg: (m, gg)),
                  pl.BlockSpec((tm, gw), lambda m, gg: (m, gg)),
                  pl.BlockSpec((tm, gw), lambda m, gg: (m, gg)),
                  pl.BlockSpec((tm, gw), lambda m, gg: (rb + m, gg)),
                  pl.BlockSpec((1, gw), lambda m, gg: (0, gg)),
                  pl.BlockSpec((1, gw), lambda m, gg: (0, gg))],
        out_specs=pl.BlockSpec((tm, gw), lambda m, gg: (m, gg)),
        out_shape=jax.ShapeDtypeStruct((nt, di), bf16),
        compiler_params=_cparams(2, 32),
        name="ssd_gate",
    )(y_f, y_b, xbc, proj, d_rep.reshape(1, di), norm_g.reshape(1, di))


def _to_slabs(v, slab_ref):
    rows, d = v.shape
    n_slab = d // LANES
    for s in range(n_slab):
        slab_ref[pl.ds(s, rows, stride=n_slab), :] = v[:, s * LANES:(s + 1) * LANES]


def _from_slabs(slab_ref, rows):
    n_slab = slab_ref.shape[0] // rows
    return jnp.concatenate([slab_ref[pl.ds(s, rows, stride=n_slab), :] for s in range(n_slab)], axis=1)


def _router_body(x_ref, sh_ref, sc_ref, w_ref, b_ref, h_ref, idx_ref, wt_ref, *, n_exp):
    h = x_ref[...] * (1.0 + sc_ref[0]) + sh_ref[0]
    _to_slabs(h, h_ref)
    tm = h.shape[0]
    scores = jax.nn.sigmoid(_dot(h.astype(bf16), w_ref[...].astype(bf16)))
    choice = scores + b_ref[...]
    n_grp = N_EXPERT_GROUPS
    per_grp = n_exp // n_grp
    s3 = scores.T[:n_exp].reshape(n_grp, per_grp, tm)
    c3 = choice.T[:n_exp].reshape(n_grp, per_grp, tm)
    ninf = -jnp.inf
    pos = lax.broadcasted_iota(jnp.int32, c3.shape, 1)
    grp = lax.broadcasted_iota(jnp.int32, c3.shape, 0)
    eidx = grp * per_grp + pos

    m1 = jnp.max(c3, axis=1, keepdims=True)
    i1 = jnp.min(jnp.where(c3 == m1, pos, per_grp), axis=1, keepdims=True)
    m2 = jnp.max(jnp.where(pos == i1, ninf, c3), axis=1, keepdims=True)
    gscore = m1 + m2
    gpos = lax.broadcasted_iota(jnp.int32, gscore.shape, 0)
    keep = jnp.zeros(gscore.shape, jnp.bool_)
    for _ in range(TOPK_GROUPS):
        m = jnp.max(gscore, axis=0, keepdims=True)
        gi = jnp.min(jnp.where(gscore == m, gpos, n_grp), axis=0, keepdims=True)
        hit = gpos == gi
        keep = jnp.logical_or(keep, hit)
        gscore = jnp.where(hit, ninf, gscore)

    cur = jnp.where(keep, c3, ninf)
    picked_w = []
    for k in range(TOP_K):
        m = jnp.max(jnp.max(cur, axis=1, keepdims=True), axis=0, keepdims=True)
        ei = jnp.min(jnp.min(jnp.where(cur == m, eidx, n_exp), axis=1, keepdims=True), axis=0, keepdims=True)
        hit = eidx == ei
        wk = jnp.sum(jnp.sum(jnp.where(hit, s3, 0.0), axis=1, keepdims=True), axis=0, keepdims=True)
        cur = jnp.where(hit, ninf, cur)
        idx_ref[k:k + 1, :] = ei.reshape(1, tm)
        picked_w.append(wk.reshape(1, tm))
    total = picked_w[0]
    for wk in picked_w[1:]:
        total = total + wk
    for k in range(TOP_K):
        wt_ref[k:k + 1, :] = picked_w[k] / total * ROUTED_SCALE


def moe_router(x, mods, rows, layer, w_router, router_bias, tm=512):
    nt, d = x.shape
    n_exp = w_router.shape[1]
    n_slab = d // LANES
    assert n_exp <= LANES and n_exp % N_EXPERT_GROUPS == 0 and d % LANES == 0
    w_pad = jnp.pad(w_router, ((0, 0), (0, LANES - n_exp)))
    b_pad = jnp.pad(router_bias, (0, LANES - n_exp)).reshape(1, LANES)
    sh = _mod_index(rows, layer, 3, tm)
    sc = _mod_index(rows, layer, 4, tm)
    return pl.pallas_call(
        functools.partial(_router_body, n_exp=n_exp),
        grid=(nt // tm,),
        in_specs=[pl.BlockSpec((tm, d), lambda m: (m, 0)),
                  pl.BlockSpec((1, 1, d), sh),
                  pl.BlockSpec((1, 1, d), sc),
                  pl.BlockSpec((d, LANES), lambda m: (0, 0)),
                  pl.BlockSpec((1, LANES), lambda m: (0, 0))],
        out_specs=[pl.BlockSpec((tm * n_slab, LANES), lambda m: (m, 0)),
                   pl.BlockSpec((TOP_K, tm), lambda m: (0, m)),
                   pl.BlockSpec((TOP_K, tm), lambda m: (0, m))],
        out_shape=[jax.ShapeDtypeStruct((nt * n_slab, LANES), f32),
                   jax.ShapeDtypeStruct((TOP_K, nt), jnp.int32),
                   jax.ShapeDtypeStruct((TOP_K, nt), f32)],
        compiler_params=_cparams(1, 32),
        name="moe_router",
    )(x, mods, mods, w_pad, b_pad)


def routing_tables(idx_t, n_exp, tmx):
    nt = idx_t.shape[1]
    tk = TOP_K * nt
    e_flat = idx_t.reshape(tk)
    _, order = lax.sort_key_val(e_flat, jnp.arange(tk, dtype=jnp.int32))
    experts = jnp.arange(n_exp, dtype=jnp.int32)
    counts = jnp.sum((e_flat[None, :] == experts[:, None]).astype(jnp.int32), axis=1)
    start = jnp.cumsum(counts) - counts
    n_blk = (counts + tmx - 1) // tmx
    blk_end = jnp.cumsum(n_blk)
    n_blocks = -(-(tk + n_exp * (tmx - 1)) // tmx)
    b = jnp.arange(n_blocks, dtype=jnp.int32)
    blk_expert = jnp.minimum(jnp.sum((blk_end[None, :] <= b[:, None]).astype(jnp.int32), axis=1), n_exp - 1)
    onehot = (blk_expert[:, None] == experts[None, :]).astype(jnp.int32)
    pick = lambda v: jnp.sum(onehot * v[None, :], axis=1)
    offset = (b - pick(blk_end - n_blk)) * tmx
    blk_rows = jnp.clip(pick(counts) - offset, 0, tmx)
    blk_src = jnp.clip(pick(start) + offset, 0, tk - 1)
    return order, blk_expert, blk_src, blk_rows


def _slab_copy(src_ref, src_slab, dst_ref, dst_slab, n_slab, sem):
    src = src_ref.at[pl.ds(pl.multiple_of(src_slab * n_slab, n_slab), n_slab), :]
    dst = dst_ref.at[pl.ds(pl.multiple_of(dst_slab * n_slab, n_slab), n_slab), :]
    return pltpu.make_async_copy(src, dst, sem)


def _expert_body(order_ref, be_ref, src_ref, rows_ref, h_hbm, wg_ref, wu_ref, wd_ref, y_hbm,
                 xbuf, ybuf, wgb, wub, wdb, gsem, ssem, *, tmx, nt, n_slab):
    i = pl.program_id(0)
    n = pl.num_programs(0)
    tk = order_ref.shape[0]
    slot = i % 2

    def start_gather(blk, s):
        base = src_ref[blk]

        def issue(r, c):
            a = order_ref[jnp.minimum(base + r, tk - 1)]
            _slab_copy(h_hbm, lax.rem(a, nt), xbuf.at[s], r, n_slab, gsem.at[s]).start()
            return c

        lax.fori_loop(0, tmx, issue, 0, unroll=8)

    def wait_scatter(blk, s):
        def drain(r, c):
            _slab_copy(ybuf.at[s], 0, y_hbm, 0, n_slab, ssem.at[s]).wait()
            return c

        lax.fori_loop(0, rows_ref[blk], drain, 0)

    @pl.when(jnp.logical_and(i == 0, rows_ref[0] > 0))
    def _():
        start_gather(0, 0)

    @pl.when(jnp.logical_and(i + 1 < n, rows_ref[jnp.minimum(i + 1, n - 1)] > 0))
    def _():
        start_gather(i + 1, 1 - slot)

    first_of_expert = jnp.logical_or(i == 0, be_ref[i] != be_ref[jnp.maximum(i - 1, 0)])

    @pl.when(jnp.logical_and(rows_ref[i] > 0, first_of_expert))
    def _():
        _cast_rows(wg_ref, wgb, 256)
        _cast_rows(wu_ref, wub, 256)
        _cast_rows(wd_ref, wdb, 256)

    @pl.when(rows_ref[i] > 0)
    def _():
        pltpu.make_async_copy(h_hbm.at[pl.ds(0, tmx * n_slab), :], xbuf.at[slot], gsem.at[slot]).wait()
        x = _from_slabs(xbuf.at[slot], tmx).astype(bf16)
        hmid = (_silu(_dot(x, wgb[...])) * _dot(x, wub[...])).astype(bf16)
        _to_slabs(_dot(hmid, wdb[...]), ybuf.at[slot])
        base = src_ref[i]

        def issue(r, c):
            _slab_copy(ybuf.at[slot], r, y_hbm, order_ref[base + r], n_slab, ssem.at[slot]).start()
            return c

        lax.fori_loop(0, rows_ref[i], issue, 0)

    @pl.when(i > 0)
    def _():
        wait_scatter(i - 1, 1 - slot)

    @pl.when(i == n - 1)
    def _():
        wait_scatter(i, slot)


def moe_experts(h_slabs, order, blk_expert, blk_src, blk_rows, w_gate, w_up, w_down, layer, tmx):
    n_blocks = blk_expert.shape[0]
    tk = order.shape[0]
    d, ff = w_gate.shape[2:]
    n_slab = d // LANES
    nt = h_slabs.shape[0] // n_slab
    wmap = lambda i, order, be, src, rows: (layer, be[i], 0, 0)
    grid_spec = pltpu.PrefetchScalarGridSpec(
        num_scalar_prefetch=4,
        grid=(n_blocks,),
        in_specs=[pl.BlockSpec(memory_space=pl.ANY),
                  pl.BlockSpec((None, None, d, ff), wmap),
                  pl.BlockSpec((None, None, d, ff), wmap),
                  pl.BlockSpec((None, None, ff, d), wmap)],
        out_specs=pl.BlockSpec(memory_space=pl.ANY),
        scratch_shapes=[pltpu.VMEM((2, tmx * n_slab, LANES), f32),
                        pltpu.VMEM((2, tmx * n_slab, LANES), f32),
                        pltpu.VMEM((d, ff), bf16), pltpu.VMEM((d, ff), bf16), pltpu.VMEM((ff, d), bf16),
                        pltpu.SemaphoreType.DMA((2,)), pltpu.SemaphoreType.DMA((2,))],
    )
    return pl.pallas_call(
        functools.partial(_expert_body, tmx=tmx, nt=nt, n_slab=n_slab),
        grid_spec=grid_spec,
        out_shape=jax.ShapeDtypeStruct((tk * n_slab, LANES), f32),
        compiler_params=_cparams(1, 48),
        name="moe_experts",
    )(order, blk_expert, blk_src, blk_rows, h_slabs, w_gate, w_up, w_down)


def _ffn_post_body(*refs, alpha):
    h_ref = refs[0]
    y_refs = refs[1:1 + TOP_K]
    wrep_ref, sg_ref, su_ref, sd_ref, x_ref, gate_ref, g_ref, b_ref, o_ref, slab_scr = refs[1 + TOP_K:]
    tm = x_ref.shape[0]
    hb = _from_slabs(h_ref, tm).astype(bf16)
    hmid = (_silu(_dot(hb, sg_ref[...])) * _dot(hb, su_ref[...])).astype(bf16)
    ffn = _dot(hmid, sd_ref[...])
    wrep = wrep_ref[...]
    acc = None
    for k in range(TOP_K):
        part = y_refs[k][...] * wrep[:, k:k + 1]
        acc = part if acc is None else acc + part
    slab_scr[...] = acc
    v = alpha * x_ref[...] + gate_ref[0] * (_from_slabs(slab_scr, tm) + ffn)
    o_ref[...] = _layer_norm(v, g_ref[...], b_ref[...])


def moe_ffn_post(h_slabs, x, y_slabs, w_t, sg_bf, su_bf, sd_bf, mods, rows, layer, ln_g, ln_b, alpha, tm=128):
    nt, d = x.shape
    ff = sg_bf.shape[1]
    n_slab = d // LANES
    n_tiles = nt // tm
    w_rep = jnp.repeat(w_t.T, n_slab, axis=0)
    gate = _mod_index(rows, layer, 5, tm)
    slab_block = (tm * n_slab, LANES)
    y_specs = [pl.BlockSpec(slab_block, lambda m, k=k: (k * n_tiles + m, 0)) for k in range(TOP_K)]
    return pl.pallas_call(
        functools.partial(_ffn_post_body, alpha=alpha),
        grid=(n_tiles,),
        in_specs=[pl.BlockSpec(slab_block, lambda m: (m, 0))] + y_specs + [
                  pl.BlockSpec((tm * n_slab, TOP_K), lambda m: (m, 0)),
                  pl.BlockSpec((d, ff), lambda m: (0, 0)),
                  pl.BlockSpec((d, ff), lambda m: (0, 0)),
                  pl.BlockSpec((ff, d), lambda m: (0, 0)),
                  pl.BlockSpec((tm, d), lambda m: (m, 0)),
                  pl.BlockSpec((1, 1, d), gate),
                  pl.BlockSpec((1, d), lambda m: (0, 0)),
                  pl.BlockSpec((1, d), lambda m: (0, 0))],
        out_specs=pl.BlockSpec((tm, d), lambda m: (m, 0)),
        out_shape=jax.ShapeDtypeStruct((nt, d), f32),
        scratch_shapes=[pltpu.VMEM(slab_block, f32)],
        compiler_params=_cparams(1, 48),
        name="moe_ffn_post",
    )(h_slabs, *([y_slabs] * TOP_K), w_rep, sg_bf, su_bf, sd_bf, x, mods, ln_g.reshape(1, d), ln_b.reshape(1, d))


MOE_ROWS_PER_BLOCK = 256


def _grid_rope(seq_len, dk):
    t = jnp.arange(seq_len)
    row = (t // GRID_W).astype(f32)
    col = (t % GRID_W).astype(f32)
    nf = dk // 4
    freqs = ROPE_THETA ** (-jnp.arange(nf, dtype=f32) / nf)
    ang = jnp.concatenate([row[:, None] * freqs, col[:, None] * freqs], axis=-1)
    return jnp.cos(ang), jnp.sin(ang)


def kernel(x_prompt, x_sample, c, state_gla, state_ssd, c_ctx, w_ada, b_ada, ln_g, ln_b,
           gla_w_in, gla_w_gate, gla_b_gate, gla_norm_g, gla_w_out,
           ssd_w_in, ssd_conv_w, ssd_conv_b, ssd_dt_bias, ssd_a_log, ssd_d, ssd_norm_g, ssd_w_out,
           moe_w_router, moe_router_bias, moe_w_gate, moe_w_up, moe_w_down,
           shared_w_gate, shared_w_up, shared_w_down):
    bp, tp, d = x_prompt.shape
    bs, ts, _ = x_sample.shape
    depth = w_ada.shape[0]
    n_prompt = bp * tp
    rows = Rows(n_prompt, ts, bs)
    alpha = (2 * depth) ** 0.25
    x = jnp.concatenate([x_prompt.reshape(n_prompt, d), x_sample.reshape(bs * ts, d)], axis=0)

    cond = jnp.zeros((8, d), f32).at[0].set(c_ctx).at[1:1 + bs].set(c)
    mods = ada_all(cond, w_ada, b_ada)[:, :1 + bs].reshape(depth * rows.mods_per_layer, 1, d)

    hk = gla_w_gate.shape[-1]
    hv = gla_w_out.shape[1]
    rope = _grid_rope(ts, hk // GLA_HEADS)
    di = ssd_w_out.shape[1]
    n_ssd_heads = ssd_dt_bias.shape[-1]
    conv_dim = ssd_conv_w.shape[-1]
    n_exp = moe_w_router.shape[-1]

    new_gla, new_ssd = [], []
    for l in range(depth):
        j = l // 2
        if l % 2 == 0:
            w_in = gla_w_in[j]
            n_main = 2 * hk + 2 * hv
            proj = mm_in(x, mods, rows, l, w_in, 0, n_main, 1024)
            w_lowrank = jnp.pad(w_in[:, n_main:], ((0, 0), (0, LANES - 2 * GLA_RANK)))
            g = mm_in(x, mods, rows, l, w_lowrank, 0, LANES, LANES)
            wg_pad = (jnp.zeros((2, LANES, hk), f32)
                      .at[0, :GLA_RANK].set(gla_w_gate[j, 0])
                      .at[1, GLA_RANK:2 * GLA_RANK].set(gla_w_gate[j, 1]))
            of_p, ob_p, st = gla_scan(proj, g, wg_pad, gla_b_gate[j], 0, bp, tp, None, None, True)
            of_s, ob_s = gla_scan(proj, g, wg_pad, gla_b_gate[j], n_prompt, bs, ts, rope, state_gla[:, j], False)
            y = jnp.concatenate([gla_gate(of_p, ob_p, proj, gla_norm_g[j], 2 * hk + hv, 0),
                                 gla_gate(of_s, ob_s, proj, gla_norm_g[j], 2 * hk + hv, n_prompt)], axis=0)
            w_out = gla_w_out[j].astype(bf16)
            new_gla.append(st)
        else:
            w_in = ssd_w_in[j]
            n_main = di + conv_dim
            proj = mm_in(x, mods, rows, l, w_in, 0, n_main, 1024)
            dt = mm_in(x, mods, rows, l, w_in, n_main, 2 * n_ssd_heads, 2 * n_ssd_heads)
            d_rep = jnp.repeat(ssd_d[j], SSD_HEADDIM)
            xbc_p = ssd_conv(proj, ssd_conv_w[j], ssd_conv_b[j], di, 0, bp, tp)
            xbc_s = ssd_conv(proj, ssd_conv_w[j], ssd_conv_b[j], di, n_prompt, bs, ts)
            yf_p, yb_p, st = ssd_scan(xbc_p, dt, ssd_dt_bias[j], ssd_a_log[j], 0, bp, tp, None, True)
            yf_s, yb_s = ssd_scan(xbc_s, dt, ssd_dt_bias[j], ssd_a_log[j], n_prompt, bs, ts, state_ssd[:, j], False)
            y = jnp.concatenate([ssd_gate(yf_p, yb_p, xbc_p, proj, d_rep, ssd_norm_g[j], 0),
                                 ssd_gate(yf_s, yb_s, xbc_s, proj, d_rep, ssd_norm_g[j], n_prompt)], axis=0)
            w_out = ssd_w_out[j].astype(bf16)
            new_ssd.append(st)
        x1 = mm_post(y, w_out, x, mods, rows, l, 2, ln_g[l, 0], ln_b[l, 0], alpha)

        h, idx_t, w_t = moe_router(x1, mods, rows, l, moe_w_router[l], moe_router_bias[l])
        order, blk_expert, blk_src, blk_rows = routing_tables(idx_t, n_exp, MOE_ROWS_PER_BLOCK)
        y_slabs = moe_experts(h, order, blk_expert, blk_src, blk_rows, moe_w_gate, moe_w_up, moe_w_down, l,
                              MOE_ROWS_PER_BLOCK)
        x = moe_ffn_post(h, x1, y_slabs, w_t, shared_w_gate[l].astype(bf16), shared_w_up[l].astype(bf16),
                         shared_w_down[l].astype(bf16), mods, rows, l, ln_g[l, 1], ln_b[l, 1], alpha)

    y_prompt = x[:n_prompt].reshape(bp, tp, d)
    y_sample = x[n_prompt:].reshape(bs, ts, d)
    return y_prompt, y_sample, jnp.stack(new_gla, axis=1), jnp.stack(new_ssd, axis=1)
```

```python
import functools
import math

import jax
import jax.numpy as jnp
from jax import lax
from jax.experimental import pallas as pl
from jax.experimental.pallas import tpu as pltpu

f32 = jnp.float32
bf16 = jnp.bfloat16

GRID_W = 64
GLA_HEADS = 4
GLA_RANK = 16
GLA_TAU = 16.0
GLA_CHUNK = 32
ROPE_THETA = 10000.0
SSD_HEADDIM = 64
SSD_STATE = 128
SSD_GROUPS = 8
N_EXPERT_GROUPS = 8
TOPK_GROUPS = 4
TOP_K = 8
ROUTED_SCALE = 2.5
LN_EPS = 1e-5
RMS_EPS = 1e-6

LANES = 128
VMEM_BYTES_V7X = 64 * 2 ** 20


def _cparams(n_axes, vmem_mib):
    assert vmem_mib * 2 ** 20 < VMEM_BYTES_V7X
    return pltpu.CompilerParams(dimension_semantics=("arbitrary",) * n_axes,
                                vmem_limit_bytes=vmem_mib * 2 ** 20)


def _dot(a, b):
    return jnp.dot(a, b, preferred_element_type=f32)


def _dot_nt(a, b):
    return lax.dot_general(a, b, (((1,), (1,)), ((), ())), preferred_element_type=f32)


def _dot_tn(a, b):
    return lax.dot_general(a, b, (((0,), (0,)), ((), ())), preferred_element_type=f32)


def _split3(x):
    hi = x.astype(bf16)
    r = x - hi.astype(f32)
    mid = r.astype(bf16)
    lo = (r - mid.astype(f32)).astype(bf16)
    return hi, mid, lo


def _dot_exact_lhs(a_bf, x):
    hi, mid, lo = _split3(x)
    return (_dot(a_bf, lo) + _dot(a_bf, mid)) + _dot(a_bf, hi)


def _dot_exact_rhs(x, e_bf):
    hi, mid, lo = _split3(x)
    return (_dot(lo, e_bf) + _dot(mid, e_bf)) + _dot(hi, e_bf)


def _silu(x):
    return x * jax.nn.sigmoid(x)


def _cast_rows(src_ref, dst_ref, rows):
    n = src_ref.shape[0]
    rows = math.gcd(n, rows)

    def body(i, c):
        r = pl.multiple_of(i * rows, rows)
        dst_ref[pl.ds(r, rows), :] = src_ref[pl.ds(r, rows), :].astype(dst_ref.dtype)
        return c

    lax.fori_loop(0, n // rows, body, 0)


def _ada_body(c_ref, w_ref, b_ref, o_ref):
    a = _silu(c_ref[...]).astype(bf16)
    o_ref[0] = _dot(a, w_ref[0].astype(bf16)) + b_ref[0]


def ada_all(cond8, w_ada, b_ada, tn=1024):
    L, D, N = w_ada.shape
    return pl.pallas_call(
        _ada_body,
        grid=(L, N // tn),
        in_specs=[pl.BlockSpec((8, D), lambda l, n: (0, 0)),
                  pl.BlockSpec((1, D, tn), lambda l, n: (l, 0, n)),
                  pl.BlockSpec((1, 1, tn), lambda l, n: (l, 0, n))],
        out_specs=pl.BlockSpec((1, 8, tn), lambda l, n: (l, 0, n)),
        out_shape=jax.ShapeDtypeStruct((L, 8, N), f32),
        compiler_params=_cparams(2, 40),
        name="ada",
    )(cond8, w_ada, b_ada.reshape(L, 1, N))


N_ADA = 6


class Rows:
    def __init__(self, n_prompt, seq_s, n_seq_s):
        self.n_prompt = n_prompt
        self.seq_s = seq_s
        self.mods_per_layer = (1 + n_seq_s) * N_ADA

    def group(self, row0):
        return jnp.where(row0 < self.n_prompt, 0, 1 + (row0 - self.n_prompt) // self.seq_s)


def _mod_index(rows, layer, j, tm):
    assert rows.n_prompt % tm == 0 and rows.seq_s % tm == 0

    def index_map(i):
        return (layer * rows.mods_per_layer + rows.group(i * tm) * N_ADA + j, 0, 0)
    return index_map


def _mm_in_body(x_ref, sh_ref, sc_ref, w_ref, o_ref, wb_ref):
    @pl.when(pl.program_id(1) == 0)
    def _():
        _cast_rows(w_ref, wb_ref, 256)

    h = (x_ref[...] * (1.0 + sc_ref[0]) + sh_ref[0]).astype(bf16)
    o_ref[...] = _dot(h, wb_ref[...])


def mm_in(x, mods, rows, layer, w, col0, n_cols, tn, tm=512):
    nt, d = x.shape
    assert n_cols % tn == 0 and col0 % tn == 0 and nt % tm == 0
    sh = _mod_index(rows, layer, 0, tm)
    sc = _mod_index(rows, layer, 1, tm)
    cb = col0 // tn
    return pl.pallas_call(
        _mm_in_body,
        grid=(n_cols // tn, nt // tm),
        in_specs=[pl.BlockSpec((tm, d), lambda n, m: (m, 0)),
                  pl.BlockSpec((1, 1, d), lambda n, m: sh(m)),
                  pl.BlockSpec((1, 1, d), lambda n, m: sc(m)),
                  pl.BlockSpec((d, tn), lambda n, m: (0, cb + n))],
        out_specs=pl.BlockSpec((tm, tn), lambda n, m: (m, n)),
        out_shape=jax.ShapeDtypeStruct((nt, n_cols), f32),
        scratch_shapes=[pltpu.VMEM((d, tn), bf16)],
        compiler_params=_cparams(2, 48),
        name="mm_in",
    )(x, mods, mods, w)


def _layer_norm(v, g, b):
    mu = jnp.mean(v, axis=-1, keepdims=True)
    d = v - mu
    var = jnp.mean(d * d, axis=-1, keepdims=True)
    return d * lax.rsqrt(var + LN_EPS) * g + b


def _mm_post_body(y_ref, w_ref, x_ref, gate_ref, g_ref, b_ref, o_ref, acc_ref, *, nk, alpha):
    k = pl.program_id(1)
    part = _dot(y_ref[...], w_ref[...])

    @pl.when(k == 0)
    def _():
        acc_ref[...] = part

    @pl.when(k > 0)
    def _():
        acc_ref[...] += part

    @pl.when(k == nk - 1)
    def _():
        v = alpha * x_ref[...] + gate_ref[0] * acc_ref[...]
        o_ref[...] = _layer_norm(v, g_ref[...], b_ref[...])


def mm_post(y, w_bf, x, mods, rows, layer, gate_j, ln_g, ln_b, alpha, tm=256, tk=2048):
    nt, kdim = y.shape
    d = x.shape[1]
    nk = kdim // tk
    gate = _mod_index(rows, layer, gate_j, tm)
    return pl.pallas_call(
        functools.partial(_mm_post_body, nk=nk, alpha=alpha),
        grid=(nt // tm, nk),
        in_specs=[pl.BlockSpec((tm, tk), lambda m, k: (m, k)),
                  pl.BlockSpec((tk, d), lambda m, k: (k, 0)),
                  pl.BlockSpec((tm, d), lambda m, k: (m, 0)),
                  pl.BlockSpec((1, 1, d), lambda m, k: gate(m)),
                  pl.BlockSpec((1, d), lambda m, k: (0, 0)),
                  pl.BlockSpec((1, d), lambda m, k: (0, 0))],
        out_specs=pl.BlockSpec((tm, d), lambda m, k: (m, 0)),
        out_shape=jax.ShapeDtypeStruct((nt, d), f32),
        scratch_shapes=[pltpu.VMEM((tm, d), f32)],
        compiler_params=_cparams(2, 48),
        name="mm_post",
    )(y, w_bf, x, mods, ln_g.reshape(1, d), ln_b.reshape(1, d))


def _chunk_masks(tb, c, reverse):
    i = lax.broadcasted_iota(jnp.int32, (tb, tb), 0)
    j = lax.broadcasted_iota(jnp.int32, (tb, tb), 1)
    same = (i // c) == (j // c)
    tri = (j >= i) if reverse else (j <= i)
    return jnp.logical_and(same, tri), same


def _gla_direction(q, k, v, g, wg, bg, rope, st_ref, reverse, c):
    tb, dk = q.shape
    assert tb % (2 * c) == 0
    causal, same = _chunk_masks(tb, c, reverse)
    causal_bf = jnp.where(causal, 1.0, 0.0).astype(bf16)
    same_bf = jnp.where(same, 1.0, 0.0).astype(bf16)

    pre = _dot(g.astype(bf16), wg.astype(bf16)) + bg
    log_a = (jnp.minimum(pre, 0.0) - jnp.log1p(jnp.exp(-jnp.abs(pre)))) * (1.0 / GLA_TAU)
    b = _dot_exact_lhs(causal_bf, log_a)
    total = _dot_exact_lhs(same_bf, log_a)

    q = q * (dk ** -0.5)
    if rope is not None:
        cos, sin = rope
        half = dk // 2
        q = jnp.concatenate([q[:, :half] * cos - q[:, half:] * sin, q[:, :half] * sin + q[:, half:] * cos], axis=1)
        k = jnp.concatenate([k[:, :half] * cos - k[:, half:] * sin, k[:, :half] * sin + k[:, half:] * cos], axis=1)

    q_loc = q * jnp.exp(b)
    k_loc = k * jnp.exp(total - b)
    q_dec = q_loc.astype(bf16)
    k_inv = (k * jnp.exp(-b)).astype(bf16)
    k_end = k_loc.astype(bf16)
    v_bf = v.astype(bf16)

    pair = 2 * c
    n_pairs = tb // pair
    sub = lax.broadcasted_iota(jnp.int32, (tb, dk), 0) // c
    scanned_first = (sub % 2) == (1 if reverse else 0)
    total_other = jnp.where(sub % 2 == 0, pltpu.roll(total, tb - c, 0), pltpu.roll(total, c, 0))
    e_other = jnp.exp(total_other)
    q_pair = (q_loc * jnp.where(scanned_first, 1.0, e_other)).astype(bf16)
    k_pair = (k_loc * jnp.where(scanned_first, e_other, 1.0)).astype(bf16)
    dec_pair = jnp.exp(total + total_other)

    i = lax.broadcasted_iota(jnp.int32, (tb, tb), 0)
    j = lax.broadcasted_iota(jnp.int32, (tb, tb), 1)
    i_second = ((i // c) % 2) == (0 if reverse else 1)
    cross = jnp.logical_and(jnp.logical_and((i // pair) == (j // pair), (i // c) != (j // c)), i_second)
    scores = jnp.where(causal, _dot_nt(q_dec, k_inv), 0.0) + jnp.where(cross, _dot_nt(q_dec, k_end), 0.0)
    o_intra = _dot(scores.astype(bf16), v_bf)

    order = range(n_pairs - 1, -1, -1) if reverse else range(n_pairs)
    o_inter = [None] * n_pairs
    st = st_ref[...]
    for pi in order:
        sl = slice(pi * pair, (pi + 1) * pair)
        o_inter[pi] = _dot_nt(q_pair[sl], st.astype(bf16))
        st = st * dec_pair[pi * pair:pi * pair + 1, :] + _dot_tn(v_bf[sl], k_pair[sl])
    st_ref[...] = st
    return o_intra + jnp.concatenate(o_inter, axis=0)


def _gla_body(*refs, c, use_rope, has_init, emit_state, nb):
    refs = list(refs)
    qf, kf, vf, gf, qb, kb, vb, gb, wgf, wgb, bgf, bgb = refs[:12]
    refs = refs[12:]
    rope_f = rope_b = None
    if use_rope:
        rope_f = (refs[0][...], refs[1][...])
        rope_b = (refs[2][...], refs[3][...])
        refs = refs[4:]
    if has_init:
        s0_ref = refs[0]
        refs = refs[1:]
    of_ref, ob_ref = refs[:2]
    refs = refs[2:]
    if emit_state:
        sfin_ref = refs[0]
        refs = refs[1:]
    stf_ref, stb_ref = refs
    t = pl.program_id(2)

    @pl.when(t == 0)
    def _():
        if has_init:
            stf_ref[...] = s0_ref[0, 0, 0].T
            stb_ref[...] = s0_ref[0, 1, 0].T
        else:
            stf_ref[...] = jnp.zeros_like(stf_ref)
            stb_ref[...] = jnp.zeros_like(stb_ref)

    of_ref[...] = _gla_direction(qf[...], kf[...], vf[...], gf[...], wgf[...], bgf[0], rope_f, stf_ref, False, c)
    ob_ref[...] = _gla_direction(qb[...], kb[...], vb[...], gb[...], wgb[...], bgb[0], rope_b, stb_ref, True, c)

    if emit_state:
        @pl.when(t == nb - 1)
        def _():
            sfin_ref[0, 0, 0] = stf_ref[...].T
            sfin_ref[0, 1, 0] = stb_ref[...].T


def gla_scan(proj, g, wg_pad, b_gate, row0, n_seq, seq_len, rope=None, s0=None, emit_state=False, tb=256):
    h = GLA_HEADS
    hk = wg_pad.shape[2]
    dk = hk // h
    dv = 2 * dk
    nb = seq_len // tb
    base = row0 // tb
    assert seq_len % tb == 0 and row0 % tb == 0 and dv == 2 * dk

    def fblk(s, t):
        return base + s * nb + t

    def bblk(s, t):
        return base + s * nb + (nb - 1 - t)

    v_off = 2 * hk // dv
    in_specs, args = [], []
    for blk in (fblk, bblk):
        in_specs += [pl.BlockSpec((tb, dk), lambda s, hh, t, blk=blk: (blk(s, t), hh)),
                     pl.BlockSpec((tb, dk), lambda s, hh, t, blk=blk: (blk(s, t), h + hh)),
                     pl.BlockSpec((tb, dv), lambda s, hh, t, blk=blk: (blk(s, t), v_off + hh)),
                     pl.BlockSpec((tb, LANES), lambda s, hh, t, blk=blk: (blk(s, t), 0))]
        args += [proj, proj, proj, g]
    in_specs += [pl.BlockSpec((None, LANES, dk), lambda s, hh, t: (0, 0, hh)),
                 pl.BlockSpec((None, LANES, dk), lambda s, hh, t: (1, 0, hh)),
                 pl.BlockSpec((None, 1, dk), lambda s, hh, t: (0, 0, hh)),
                 pl.BlockSpec((None, 1, dk), lambda s, hh, t: (1, 0, hh))]
    bg3 = b_gate.reshape(2, 1, hk)
    args += [wg_pad, wg_pad, bg3, bg3]
    if rope is not None:
        for tab_blk in (lambda s, hh, t: (t, 0), lambda s, hh, t: (nb - 1 - t, 0)):
            in_specs += [pl.BlockSpec((tb, dk // 2), tab_blk), pl.BlockSpec((tb, dk // 2), tab_blk)]
            args += [rope[0], rope[1]]
    if s0 is not None:
        in_specs.append(pl.BlockSpec((1, 2, 1, dk, dv), lambda s, hh, t: (s, 0, hh, 0, 0)))
        args.append(s0)
    n_rows = n_seq * seq_len
    out_specs = [pl.BlockSpec((tb, dv), lambda s, hh, t: (s * nb + t, hh)),
                 pl.BlockSpec((tb, dv), lambda s, hh, t: (s * nb + (nb - 1 - t), hh))]
    out_shape = [jax.ShapeDtypeStruct((n_rows, h * dv), f32)] * 2
    if emit_state:
        out_specs.append(pl.BlockSpec((1, 2, 1, dk, dv), lambda s, hh, t: (s, 0, hh, 0, 0)))
        out_shape.append(jax.ShapeDtypeStruct((n_seq, 2, h, dk, dv), f32))
    return pl.pallas_call(
        functools.partial(_gla_body, c=GLA_CHUNK, use_rope=rope is not None, has_init=s0 is not None,
                          emit_state=emit_state, nb=nb),
        grid=(n_seq, h, nb),
        in_specs=in_specs,
        out_specs=out_specs,
        out_shape=out_shape,
        scratch_shapes=[pltpu.VMEM((dv, dk), f32), pltpu.VMEM((dv, dk), f32)],
        compiler_params=_cparams(3, 40),
        name="gla_scan",
    )(*args)


def _gla_gate_body(of_ref, ob_ref, r_ref, g_ref, y_ref):
    o = of_ref[...] + ob_ref[...]
    o = o * lax.rsqrt(jnp.mean(o * o, axis=-1, keepdims=True) + RMS_EPS) * g_ref[...]
    y_ref[...] = (o * _silu(r_ref[...])).astype(y_ref.dtype)


def gla_gate(o_f, o_b, proj, norm_g, r_col0, row0, tm=512):
    nt, hv = o_f.shape
    dv = norm_g.shape[0]
    r_off = r_col0 // dv
    rb = row0 // tm
    assert row0 % tm == 0 and nt % tm == 0
    return pl.pallas_call(
        _gla_gate_body,
        grid=(nt // tm, hv // dv),
        in_specs=[pl.BlockSpec((tm, dv), lambda m, hh: (m, hh)),
                  pl.BlockSpec((tm, dv), lambda m, hh: (m, hh)),
                  pl.BlockSpec((tm, dv), lambda m, hh: (rb + m, r_off + hh)),
                  pl.BlockSpec((1, dv), lambda m, hh: (0, 0))],
        out_specs=pl.BlockSpec((tm, dv), lambda m, hh: (m, hh)),
        out_shape=jax.ShapeDtypeStruct((nt, hv), bf16),
        compiler_params=_cparams(2, 32),
        name="gla_gate",
    )(o_f, o_b, proj, norm_g.reshape(1, dv))


def _conv_body(x_ref, w_ref, b_ref, o_ref):
    x = x_ref[...]
    t = x.shape[0]
    row = lax.broadcasted_iota(jnp.int32, x.shape, 0)
    prev = jnp.where(row == 0, 0.0, pltpu.roll(x, 1, 0))
    nxt = jnp.where(row == t - 1, 0.0, pltpu.roll(x, t - 1, 0))
    w = w_ref[...]
    o_ref[...] = _silu(prev * w[0:1] + x * w[1:2] + nxt * w[2:3] + b_ref[...])


CONV_BLOCK_BYTES = 2 * 2 ** 20


def ssd_conv(proj, conv_w, conv_b, col0, row0, n_seq, seq_len):
    nc = conv_w.shape[1]
    tc = max(LANES, min(4 * LANES, CONV_BLOCK_BYTES // (4 * seq_len) // LANES * LANES))
    assert col0 % tc == 0 and nc % tc == 0 and row0 % seq_len == 0
    cb, rb = col0 // tc, row0 // seq_len
    return pl.pallas_call(
        _conv_body,
        grid=(n_seq, nc // tc),
        in_specs=[pl.BlockSpec((seq_len, tc), lambda s, ci: (rb + s, cb + ci)),
                  pl.BlockSpec((3, tc), lambda s, ci: (0, ci)),
                  pl.BlockSpec((1, tc), lambda s, ci: (0, ci))],
        out_specs=pl.BlockSpec((seq_len, tc), lambda s, ci: (s, ci)),
        out_shape=jax.ShapeDtypeStruct((n_seq * seq_len, nc), f32),
        compiler_params=_cparams(2, 48),
        name="ssd_conv",
    )(proj, conv_w, conv_b.reshape(1, nc))


def _softplus(x):
    return jnp.maximum(x, 0.0) + jnp.log1p(jnp.exp(-jnp.abs(x)))


def _ssd_direction(x, bm, cm, dt_raw, bias, a_row, e_wide, e_head, ht_ref, reverse):
    tb, width = x.shape
    p = SSD_HEADDIM
    n_heads = width // p
    i = lax.broadcasted_iota(jnp.int32, (tb, tb), 0)
    j = lax.broadcasted_iota(jnp.int32, (tb, tb), 1)
    causal = (j >= i) if reverse else (j <= i)
    causal_bf = jnp.where(causal, 1.0, 0.0).astype(bf16)

    dt = _softplus(dt_raw + bias)
    cs = _dot_exact_lhs(causal_bf, dt * a_row)
    cs_w = _dot_exact_rhs(cs, e_wide)
    dt_w = _dot_exact_rhs(dt, e_wide)
    cs_h = _dot_exact_rhs(cs, e_head)
    cs_ht = cs_h.T
    last = 0 if reverse else tb - 1
    cs_last = cs_w[last:last + 1, :]

    bm_bf = bm.astype(bf16)
    cm_bf = cm.astype(bf16)
    cb = _dot_nt(cm_bf, bm_bf)
    xdt = (x * dt_w).astype(bf16)
    lane = lax.broadcasted_iota(jnp.int32, (tb, LANES), 1)
    heads_per_slab = LANES // p
    y_slabs = []
    for sidx in range(width // LANES):
        xs = xdt[:, sidx * LANES:(sidx + 1) * LANES]
        acc = None
        for hs in range(heads_per_slab):
            hh = sidx * heads_per_slab + hs
            diff = cs_h[:, hh:hh + 1] - cs_ht[hh:hh + 1, :]
            m = (cb * jnp.exp(jnp.where(causal, diff, -jnp.inf))).astype(bf16)
            part = _dot(m, jnp.where((lane // p) == hs, xs, jnp.zeros_like(xs)))
            acc = part if acc is None else acc + part
        y_slabs.append(acc)
    ht = ht_ref[...]
    y = jnp.concatenate(y_slabs, axis=1) + _dot(cm_bf, ht.astype(bf16)) * jnp.exp(cs_w)
    xw = (x * (jnp.exp(cs_last - cs_w) * dt_w)).astype(bf16)
    ht_ref[...] = ht * jnp.exp(cs_last) + _dot_tn(bm_bf, xw)
    return y


def _ssd_body(*refs, has_init, emit_state, nb):
    refs = list(refs)
    xf, bf_, cf, dtf, xb, bb, cb_, dtb, bias, alog, ewf, ewb, ehf, ehb = refs[:14]
    refs = refs[14:]
    if has_init:
        s0_ref = refs[0]
        refs = refs[1:]
    yf_ref, yb_ref = refs[:2]
    refs = refs[2:]
    if emit_state:
        sfin_ref = refs[0]
        refs = refs[1:]
    htf_ref, htb_ref = refs
    t = pl.program_id(2)
    n, width = htf_ref.shape

    @pl.when(t == 0)
    def _():
        if has_init:
            htf_ref[...] = s0_ref[0, 0].reshape(width, n).T
            htb_ref[...] = s0_ref[0, 1].reshape(width, n).T
        else:
            htf_ref[...] = jnp.zeros_like(htf_ref)
            htb_ref[...] = jnp.zeros_like(htb_ref)

    a_row = -jnp.exp(alog[...])
    yf_ref[...] = _ssd_direction(xf[...], bf_[...], cf[...], dtf[...], bias[...], a_row, ewf[...], ehf[...],
                                 htf_ref, False)
    yb_ref[...] = _ssd_direction(xb[...], bb[...], cb_[...], dtb[...], bias[...], a_row, ewb[...], ehb[...],
                                 htb_ref, True)

    if emit_state:
        @pl.when(t == nb - 1)
        def _():
            sfin_ref[0, 0] = htf_ref[...].T.reshape(sfin_ref.shape[2:])
            sfin_ref[0, 1] = htb_ref[...].T.reshape(sfin_ref.shape[2:])


def _ssd_select_matrices(n_heads):
    hg = n_heads // SSD_GROUPS
    col = jnp.arange(2 * n_heads)[None, None, :, None]
    d = jnp.arange(2)[:, None, None, None]
    g = jnp.arange(SSD_GROUPS)[None, :, None, None]
    wide = jnp.arange(hg * SSD_HEADDIM)[None, None, None, :] // SSD_HEADDIM
    head = jnp.arange(LANES)[None, None, None, :]
    first = d * n_heads + g * hg
    e_wide = (col == first + wide).astype(bf16)
    e_head = jnp.logical_and(col == first + head, head < hg).astype(bf16)
    return e_wide, e_head


def ssd_scan(xbc, dt, dt_bias, a_log, row0, n_seq, seq_len, s0=None, emit_state=False, tb=256):
    n_heads = dt_bias.shape[1]
    assert 2 * n_heads == LANES
    hg = n_heads // SSD_GROUPS
    p, n = SSD_HEADDIM, SSD_STATE
    width = hg * p
    di = n_heads * p
    nb = seq_len // tb
    base = row0 // tb
    assert seq_len % tb == 0 and row0 % tb == 0 and n == LANES
    e_wide, e_head = _ssd_select_matrices(n_heads)
    b_off = di // n
    c_off = b_off + SSD_GROUPS

    in_specs, args = [], []
    for rev in (False, True):
        def blk(s, t, rev=rev):
            return s * nb + ((nb - 1 - t) if rev else t)
        in_specs += [pl.BlockSpec((tb, width), lambda s, g, t, blk=blk: (blk(s, t), g)),
                     pl.BlockSpec((tb, n), lambda s, g, t, blk=blk: (blk(s, t), b_off + g)),
                     pl.BlockSpec((tb, n), lambda s, g, t, blk=blk: (blk(s, t), c_off + g)),
                     pl.BlockSpec((tb, LANES), lambda s, g, t, blk=blk: (base + blk(s, t), 0))]
        args += [xbc, xbc, xbc, dt]
    in_specs += [pl.BlockSpec((1, LANES), lambda s, g, t: (0, 0)),
                 pl.BlockSpec((1, LANES), lambda s, g, t: (0, 0)),
                 pl.BlockSpec((None, None, LANES, width), lambda s, g, t: (0, g, 0, 0)),
                 pl.BlockSpec((None, None, LANES, width), lambda s, g, t: (1, g, 0, 0)),
                 pl.BlockSpec((None, None, LANES, LANES), lambda s, g, t: (0, g, 0, 0)),
                 pl.BlockSpec((None, None, LANES, LANES), lambda s, g, t: (1, g, 0, 0))]
    args += [dt_bias.reshape(1, LANES), a_log.reshape(1, LANES), e_wide, e_wide, e_head, e_head]
    if s0 is not None:
        in_specs.append(pl.BlockSpec((1, 2, hg, p, n), lambda s, g, t: (s, 0, g, 0, 0)))
        args.append(s0)
    n_rows = n_seq * seq_len
    out_specs = [pl.BlockSpec((tb, width), lambda s, g, t: (s * nb + t, g)),
                 pl.BlockSpec((tb, width), lambda s, g, t: (s * nb + (nb - 1 - t), g))]
    out_shape = [jax.ShapeDtypeStruct((n_rows, di), f32)] * 2
    if emit_state:
        out_specs.append(pl.BlockSpec((1, 2, hg, p, n), lambda s, g, t: (s, 0, g, 0, 0)))
        out_shape.append(jax.ShapeDtypeStruct((n_seq, 2, n_heads, p, n), f32))
    return pl.pallas_call(
        functools.partial(_ssd_body, has_init=s0 is not None, emit_state=emit_state, nb=nb),
        grid=(n_seq, SSD_GROUPS, nb),
        in_specs=in_specs,
        out_specs=out_specs,
        out_shape=out_shape,
        scratch_shapes=[pltpu.VMEM((n, width), f32), pltpu.VMEM((n, width), f32)],
        compiler_params=_cparams(3, 40),
        name="ssd_scan",
    )(*args)


def _ssd_gate_body(yf_ref, yb_ref, xs_ref, z_ref, d_ref, g_ref, o_ref):
    y = yf_ref[...] + yb_ref[...] + d_ref[...] * xs_ref[...]
    y = y * _silu(z_ref[...])
    y = y * lax.rsqrt(jnp.mean(y * y, axis=-1, keepdims=True) + RMS_EPS)
    o_ref[...] = (y * g_ref[...]).astype(o_ref.dtype)


def ssd_gate(y_f, y_b, xbc, proj, d_rep, norm_g, row0, tm=512):
    nt, di = y_f.shape
    gw = di // SSD_GROUPS
    rb = row0 // tm
    assert row0 % tm == 0 and nt % tm == 0
    return pl.pallas_call(
        _ssd_gate_body,
        grid=(nt // tm, SSD_GROUPS),
        in_specs=[pl.BlockSpec((tm, gw), lambda m, gg: (m, gg)),
                  pl.BlockSpec((tm, gw), lambda m, gg: (m, gg)),
                  pl.BlockSpec((tm, gw), lambda m, gg: (m, gg)),
                  pl.BlockSpec((tm, gw), lambda m, gg: (rb + m, gg)),
                  pl.BlockSpec((1, gw), lambda m, gg: (0, gg)),
                  pl.BlockSpec((1, gw), lambda m, gg: (0, gg))],
        out_specs=pl.BlockSpec((tm, gw), lambda m, gg: (m, gg)),
        out_shape=jax.ShapeDtypeStruct((nt, di), bf16),
        compiler_params=_cparams(2, 32),
        name="ssd_gate",
    )(y_f, y_b, xbc, proj, d_rep.reshape(1, di), norm_g.reshape(1, di))


def _to_slabs(v, slab_ref):
    rows, d = v.shape
    n_slab = d // LANES
    for s in range(n_slab):
        slab_ref[pl.ds(s, rows, stride=n_slab), :] = v[:, s * LANES:(s + 1) * LANES]


def _from_slabs(slab_ref, rows):
    n_slab = slab_ref.shape[0] // rows
    return jnp.concatenate([slab_ref[pl.ds(s, rows, stride=n_slab), :] for s in range(n_slab)], axis=1)


def _router_body(x_ref, sh_ref, sc_ref, w_ref, b_ref, h_ref, idx_ref, wt_ref, *, n_exp):
    h = x_ref[...] * (1.0 + sc_ref[0]) + sh_ref[0]
    _to_slabs(h, h_ref)
    tm = h.shape[0]
    scores = jax.nn.sigmoid(_dot(h.astype(bf16), w_ref[...].astype(bf16)))
    choice = scores + b_ref[...]
    n_grp = N_EXPERT_GROUPS
    per_grp = n_exp // n_grp
    s3 = scores.T[:n_exp].reshape(n_grp, per_grp, tm)
    c3 = choice.T[:n_exp].reshape(n_grp, per_grp, tm)
    ninf = -jnp.inf
    pos = lax.broadcasted_iota(jnp.int32, c3.shape, 1)
    grp = lax.broadcasted_iota(jnp.int32, c3.shape, 0)
    eidx = grp * per_grp + pos

    m1 = jnp.max(c3, axis=1, keepdims=True)
    i1 = jnp.min(jnp.where(c3 == m1, pos, per_grp), axis=1, keepdims=True)
    m2 = jnp.max(jnp.where(pos == i1, ninf, c3), axis=1, keepdims=True)
    gscore = m1 + m2
    gpos = lax.broadcasted_iota(jnp.int32, gscore.shape, 0)
    keep = jnp.zeros(gscore.shape, jnp.bool_)
    for _ in range(TOPK_GROUPS):
        m = jnp.max(gscore, axis=0, keepdims=True)
        gi = jnp.min(jnp.where(gscore == m, gpos, n_grp), axis=0, keepdims=True)
        hit = gpos == gi
        keep = jnp.logical_or(keep, hit)
        gscore = jnp.where(hit, ninf, gscore)

    cur = jnp.where(keep, c3, ninf)
    picked_w = []
    for k in range(TOP_K):
        m = jnp.max(jnp.max(cur, axis=1, keepdims=True), axis=0, keepdims=True)
        ei = jnp.min(jnp.min(jnp.where(cur == m, eidx, n_exp), axis=1, keepdims=True), axis=0, keepdims=True)
        hit = eidx == ei
        wk = jnp.sum(jnp.sum(jnp.where(hit, s3, 0.0), axis=1, keepdims=True), axis=0, keepdims=True)
        cur = jnp.where(hit, ninf, cur)
        idx_ref[k:k + 1, :] = ei.reshape(1, tm)
        picked_w.append(wk.reshape(1, tm))
    total = picked_w[0]
    for wk in picked_w[1:]:
        total = total + wk
    for k in range(TOP_K):
        wt_ref[k:k + 1, :] = picked_w[k] / total * ROUTED_SCALE


def moe_router(x, mods, rows, layer, w_router, router_bias, tm=512):
    nt, d = x.shape
    n_exp = w_router.shape[1]
    n_slab = d // LANES
    assert n_exp <= LANES and n_exp % N_EXPERT_GROUPS == 0 and d % LANES == 0
    w_pad = jnp.pad(w_router, ((0, 0), (0, LANES - n_exp)))
    b_pad = jnp.pad(router_bias, (0, LANES - n_exp)).reshape(1, LANES)
    sh = _mod_index(rows, layer, 3, tm)
    sc = _mod_index(rows, layer, 4, tm)
    return pl.pallas_call(
        functools.partial(_router_body, n_exp=n_exp),
        grid=(nt // tm,),
        in_specs=[pl.BlockSpec((tm, d), lambda m: (m, 0)),
                  pl.BlockSpec((1, 1, d), sh),
                  pl.BlockSpec((1, 1, d), sc),
                  pl.BlockSpec((d, LANES), lambda m: (0, 0)),
                  pl.BlockSpec((1, LANES), lambda m: (0, 0))],
        out_specs=[pl.BlockSpec((tm * n_slab, LANES), lambda m: (m, 0)),
                   pl.BlockSpec((TOP_K, tm), lambda m: (0, m)),
                   pl.BlockSpec((TOP_K, tm), lambda m: (0, m))],
        out_shape=[jax.ShapeDtypeStruct((nt * n_slab, LANES), f32),
                   jax.ShapeDtypeStruct((TOP_K, nt), jnp.int32),
                   jax.ShapeDtypeStruct((TOP_K, nt), f32)],
        compiler_params=_cparams(1, 32),
        name="moe_router",
    )(x, mods, mods, w_pad, b_pad)


def routing_tables(idx_t, n_exp, tmx):
    nt = idx_t.shape[1]
    tk = TOP_K * nt
    e_flat = idx_t.reshape(tk)
    _, order = lax.sort_key_val(e_flat, jnp.arange(tk, dtype=jnp.int32))
    experts = jnp.arange(n_exp, dtype=jnp.int32)
    counts = jnp.sum((e_flat[None, :] == experts[:, None]).astype(jnp.int32), axis=1)
    start = jnp.cumsum(counts) - counts
    n_blk = (counts + tmx - 1) // tmx
    blk_end = jnp.cumsum(n_blk)
    n_blocks = -(-(tk + n_exp * (tmx - 1)) // tmx)
    b = jnp.arange(n_blocks, dtype=jnp.int32)
    blk_expert = jnp.minimum(jnp.sum((blk_end[None, :] <= b[:, None]).astype(jnp.int32), axis=1), n_exp - 1)
    onehot = (blk_expert[:, None] == experts[None, :]).astype(jnp.int32)
    pick = lambda v: jnp.sum(onehot * v[None, :], axis=1)
    offset = (b - pick(blk_end - n_blk)) * tmx
    blk_rows = jnp.clip(pick(counts) - offset, 0, tmx)
    blk_src = jnp.clip(pick(start) + offset, 0, tk - 1)
    return order, blk_expert, blk_src, blk_rows


def _slab_copy(src_ref, src_slab, dst_ref, dst_slab, n_slab, sem):
    src = src_ref.at[pl.ds(pl.multiple_of(src_slab * n_slab, n_slab), n_slab), :]
    dst = dst_ref.at[pl.ds(pl.multiple_of(dst_slab * n_slab, n_slab), n_slab), :]
    return pltpu.make_async_copy(src, dst, sem)


def _expert_body(order_ref, be_ref, src_ref, rows_ref, h_hbm, wg_ref, wu_ref, wd_ref, y_hbm,
                 xbuf, ybuf, xs_ref, wgb, wub, wdb, gsem, ssem, *, tmx, nt, n_slab):
    i = pl.program_id(0)
    n = pl.num_programs(0)
    tk = order_ref.shape[0]
    slot = i % 2
    used = rows_ref[i] > 0

    def gather_copy(base, r, s):
        a = order_ref[jnp.minimum(base + r, tk - 1)]
        return _slab_copy(h_hbm, lax.rem(a, nt), xbuf.at[s], r, n_slab, gsem.at[s])

    def scatter_copy(base, n_real, r, s):
        a = jnp.where(r < n_real, order_ref[jnp.minimum(base + r, tk - 1)], tk + s * tmx + r)
        return _slab_copy(ybuf.at[s], r, y_hbm, a, n_slab, ssem.at[s])

    def wait_gather(s):
        pltpu.make_async_copy(h_hbm.at[pl.ds(0, tmx * n_slab), :], xbuf.at[s], gsem.at[s]).wait()

    def wait_scatter(s):
        pltpu.make_async_copy(ybuf.at[s], y_hbm.at[pl.ds(0, tmx * n_slab), :], ssem.at[s]).wait()

    @pl.when(i == 0)
    def _():
        ybuf[...] = jnp.zeros_like(ybuf)
        for s in range(2):
            spare = y_hbm.at[pl.ds((tk + s * tmx) * n_slab, tmx * n_slab), :]
            fill = pltpu.make_async_copy(ybuf.at[s], spare, ssem.at[s])
            fill.start()
            fill.wait()

        def issue(r, c):
            gather_copy(src_ref[0], r, 0).start()
            return c

        lax.fori_loop(0, tmx, issue, 0, unroll=8)

    first_of_expert = jnp.logical_or(i == 0, be_ref[i] != be_ref[jnp.maximum(i - 1, 0)])

    @pl.when(jnp.logical_and(used, first_of_expert))
    def _():
        _cast_rows(wg_ref, wgb, 256)
        _cast_rows(wu_ref, wub, 256)
        _cast_rows(wd_ref, wdb, 256)

    has_next = jnp.logical_and(i + 1 < n, rows_ref[jnp.minimum(i + 1, n - 1)] > 0)
    nxt = jnp.where(has_next, i + 1, i)
    prev = jnp.maximum(i - 1, 0)
    prev_rows = jnp.where(i > 0, rows_ref[prev], 0)

    @pl.when(used)
    def _():
        wait_gather(slot)
        xs_ref[...] = _from_slabs(xbuf.at[slot], tmx).astype(bf16)
        nxt_base = src_ref[nxt]
        prev_base = src_ref[prev]
        for r in range(tmx):
            gather_copy(nxt_base, r, 1 - slot).start()
            scatter_copy(prev_base, prev_rows, r, 1 - slot).start()
        x = xs_ref[...]
        hmid = (_silu(_dot(x, wgb[...])) * _dot(x, wub[...])).astype(bf16)
        _to_slabs(_dot(hmid, wdb[...]), ybuf.at[slot])
        wait_scatter(1 - slot)

    @pl.when(jnp.logical_and(used, jnp.logical_not(has_next)))
    def _():
        wait_gather(1 - slot)
        base, n_real = src_ref[i], rows_ref[i]

        def issue(r, c):
            scatter_copy(base, n_real, r, slot).start()
            return c

        lax.fori_loop(0, tmx, issue, 0, unroll=8)
        wait_scatter(slot)


def moe_experts(h_slabs, order, blk_expert, blk_src, blk_rows, w_gate, w_up, w_down, layer, tmx):
    n_blocks = blk_expert.shape[0]
    tk = order.shape[0]
    d, ff = w_gate.shape[2:]
    n_slab = d // LANES
    nt = h_slabs.shape[0] // n_slab
    wmap = lambda i, order, be, src, rows: (layer, be[i], 0, 0)
    grid_spec = pltpu.PrefetchScalarGridSpec(
        num_scalar_prefetch=4,
        grid=(n_blocks,),
        in_specs=[pl.BlockSpec(memory_space=pl.ANY),
                  pl.BlockSpec((None, None, d, ff), wmap),
                  pl.BlockSpec((None, None, d, ff), wmap),
                  pl.BlockSpec((None, None, ff, d), wmap)],
        out_specs=pl.BlockSpec(memory_space=pl.ANY),
        scratch_shapes=[pltpu.VMEM((2, tmx * n_slab, LANES), f32),
                        pltpu.VMEM((2, tmx * n_slab, LANES), f32),
                        pltpu.VMEM((tmx, d), bf16),
                        pltpu.VMEM((d, ff), bf16), pltpu.VMEM((d, ff), bf16), pltpu.VMEM((ff, d), bf16),
                        pltpu.SemaphoreType.DMA((2,)), pltpu.SemaphoreType.DMA((2,))],
    )
    return pl.pallas_call(
        functools.partial(_expert_body, tmx=tmx, nt=nt, n_slab=n_slab),
        grid_spec=grid_spec,
        out_shape=jax.ShapeDtypeStruct(((tk + 2 * tmx) * n_slab, LANES), f32),
        compiler_params=_cparams(1, 48),
        name="moe_experts",
    )(order, blk_expert, blk_src, blk_rows, h_slabs, w_gate, w_up, w_down)


def _ffn_post_body(*refs, alpha):
    h_ref = refs[0]
    y_refs = refs[1:1 + TOP_K]
    wrep_ref, sg_ref, su_ref, sd_ref, x_ref, gate_ref, g_ref, b_ref, o_ref, slab_scr = refs[1 + TOP_K:]
    tm = x_ref.shape[0]
    hb = _from_slabs(h_ref, tm).astype(bf16)
    hmid = (_silu(_dot(hb, sg_ref[...])) * _dot(hb, su_ref[...])).astype(bf16)
    ffn = _dot(hmid, sd_ref[...])
    wrep = wrep_ref[...]
    acc = None
    for k in range(TOP_K):
        part = y_refs[k][...] * wrep[:, k:k + 1]
        acc = part if acc is None else acc + part
    slab_scr[...] = acc
    v = alpha * x_ref[...] + gate_ref[0] * (_from_slabs(slab_scr, tm) + ffn)
    o_ref[...] = _layer_norm(v, g_ref[...], b_ref[...])


def moe_ffn_post(h_slabs, x, y_slabs, w_t, sg_bf, su_bf, sd_bf, mods, rows, layer, ln_g, ln_b, alpha, tm=128):
    nt, d = x.shape
    ff = sg_bf.shape[1]
    n_slab = d // LANES
    n_tiles = nt // tm
    w_rep = jnp.repeat(w_t.T, n_slab, axis=0)
    gate = _mod_index(rows, layer, 5, tm)
    slab_block = (tm * n_slab, LANES)
    y_specs = [pl.BlockSpec(slab_block, lambda m, k=k: (k * n_tiles + m, 0)) for k in range(TOP_K)]
    return pl.pallas_call(
        functools.partial(_ffn_post_body, alpha=alpha),
        grid=(n_tiles,),
        in_specs=[pl.BlockSpec(slab_block, lambda m: (m, 0))] + y_specs + [
                  pl.BlockSpec((tm * n_slab, TOP_K), lambda m: (m, 0)),
                  pl.BlockSpec((d, ff), lambda m: (0, 0)),
                  pl.BlockSpec((d, ff), lambda m: (0, 0)),
                  pl.BlockSpec((ff, d), lambda m: (0, 0)),
                  pl.BlockSpec((tm, d), lambda m: (m, 0)),
                  pl.BlockSpec((1, 1, d), gate),
                  pl.BlockSpec((1, d), lambda m: (0, 0)),
                  pl.BlockSpec((1, d), lambda m: (0, 0))],
        out_specs=pl.BlockSpec((tm, d), lambda m: (m, 0)),
        out_shape=jax.ShapeDtypeStruct((nt, d), f32),
        scratch_shapes=[pltpu.VMEM(slab_block, f32)],
        compiler_params=_cparams(1, 48),
        name="moe_ffn_post",
    )(h_slabs, *([y_slabs] * TOP_K), w_rep, sg_bf, su_bf, sd_bf, x, mods, ln_g.reshape(1, d), ln_b.reshape(1, d))


MOE_ROWS_PER_BLOCK = 256


def _grid_rope(seq_len, dk):
    t = jnp.arange(seq_len)
    row = (t // GRID_W).astype(f32)
    col = (t % GRID_W).astype(f32)
    nf = dk // 4
    freqs = ROPE_THETA ** (-jnp.arange(nf, dtype=f32) / nf)
    ang = jnp.concatenate([row[:, None] * freqs, col[:, None] * freqs], axis=-1)
    return jnp.cos(ang), jnp.sin(ang)


def kernel(x_prompt, x_sample, c, state_gla, state_ssd, c_ctx, w_ada, b_ada, ln_g, ln_b,
           gla_w_in, gla_w_gate, gla_b_gate, gla_norm_g, gla_w_out,
           ssd_w_in, ssd_conv_w, ssd_conv_b, ssd_dt_bias, ssd_a_log, ssd_d, ssd_norm_g, ssd_w_out,
           moe_w_router, moe_router_bias, moe_w_gate, moe_w_up, moe_w_down,
           shared_w_gate, shared_w_up, shared_w_down):
    bp, tp, d = x_prompt.shape
    bs, ts, _ = x_sample.shape
    depth = w_ada.shape[0]
    n_prompt = bp * tp
    rows = Rows(n_prompt, ts, bs)
    alpha = (2 * depth) ** 0.25
    x = jnp.concatenate([x_prompt.reshape(n_prompt, d), x_sample.reshape(bs * ts, d)], axis=0)

    cond = jnp.zeros((8, d), f32).at[0].set(c_ctx).at[1:1 + bs].set(c)
    mods = ada_all(cond, w_ada, b_ada)[:, :1 + bs].reshape(depth * rows.mods_per_layer, 1, d)

    hk = gla_w_gate.shape[-1]
    hv = gla_w_out.shape[1]
    rope = _grid_rope(ts, hk // GLA_HEADS)
    di = ssd_w_out.shape[1]
    n_ssd_heads = ssd_dt_bias.shape[-1]
    conv_dim = ssd_conv_w.shape[-1]
    n_exp = moe_w_router.shape[-1]

    new_gla, new_ssd = [], []
    for l in range(depth):
        j = l // 2
        if l % 2 == 0:
            w_in = gla_w_in[j]
            n_main = 2 * hk + 2 * hv
            proj = mm_in(x, mods, rows, l, w_in, 0, n_main, 1024)
            w_lowrank = jnp.pad(w_in[:, n_main:], ((0, 0), (0, LANES - 2 * GLA_RANK)))
            g = mm_in(x, mods, rows, l, w_lowrank, 0, LANES, LANES)
            wg_pad = (jnp.zeros((2, LANES, hk), f32)
                      .at[0, :GLA_RANK].set(gla_w_gate[j, 0])
                      .at[1, GLA_RANK:2 * GLA_RANK].set(gla_w_gate[j, 1]))
            of_p, ob_p, st = gla_scan(proj, g, wg_pad, gla_b_gate[j], 0, bp, tp, None, None, True)
            of_s, ob_s = gla_scan(proj, g, wg_pad, gla_b_gate[j], n_prompt, bs, ts, rope, state_gla[:, j], False)
            y = jnp.concatenate([gla_gate(of_p, ob_p, proj, gla_norm_g[j], 2 * hk + hv, 0),
                                 gla_gate(of_s, ob_s, proj, gla_norm_g[j], 2 * hk + hv, n_prompt)], axis=0)
            w_out = gla_w_out[j].astype(bf16)
            new_gla.append(st)
        else:
            w_in = ssd_w_in[j]
            n_main = di + conv_dim
            proj = mm_in(x, mods, rows, l, w_in, 0, n_main, 1024)
            dt = mm_in(x, mods, rows, l, w_in, n_main, 2 * n_ssd_heads, 2 * n_ssd_heads)
            d_rep = jnp.repeat(ssd_d[j], SSD_HEADDIM)
            xbc_p = ssd_conv(proj, ssd_conv_w[j], ssd_conv_b[j], di, 0, bp, tp)
            xbc_s = ssd_conv(proj, ssd_conv_w[j], ssd_conv_b[j], di, n_prompt, bs, ts)
            yf_p, yb_p, st = ssd_scan(xbc_p, dt, ssd_dt_bias[j], ssd_a_log[j], 0, bp, tp, None, True)
            yf_s, yb_s = ssd_scan(xbc_s, dt, ssd_dt_bias[j], ssd_a_log[j], n_prompt, bs, ts, state_ssd[:, j], False)
            y = jnp.concatenate([ssd_gate(yf_p, yb_p, xbc_p, proj, d_rep, ssd_norm_g[j], 0),
                                 ssd_gate(yf_s, yb_s, xbc_s, proj, d_rep, ssd_norm_g[j], n_prompt)], axis=0)
            w_out = ssd_w_out[j].astype(bf16)
            new_ssd.append(st)
        x1 = mm_post(y, w_out, x, mods, rows, l, 2, ln_g[l, 0], ln_b[l, 0], alpha)

        h, idx_t, w_t = moe_router(x1, mods, rows, l, moe_w_router[l], moe_router_bias[l])
        order, blk_expert, blk_src, blk_rows = routing_tables(idx_t, n_exp, MOE_ROWS_PER_BLOCK)
        y_slabs = moe_experts(h, order, blk_expert, blk_src, blk_rows, moe_w_gate, moe_w_up, moe_w_down, l,
                              MOE_ROWS_PER_BLOCK)
        x = moe_ffn_post(h, x1, y_slabs, w_t, shared_w_gate[l].astype(bf16), shared_w_up[l].astype(bf16),
                         shared_w_down[l].astype(bf16), mods, rows, l, ln_g[l, 1], ln_b[l, 1], alpha)

    y_prompt = x[:n_prompt].reshape(bp, tp, d)
    y_sample = x[n_prompt:].reshape(bs, ts, d)
    return y_prompt, y_sample, jnp.stack(new_gla, axis=1), jnp.stack(new_ssd, axis=1)
```

```python
import functools
import math

import jax
import jax.numpy as jnp
from jax import lax
from jax.experimental import pallas as pl
from jax.experimental.pallas import tpu as pltpu

f32 = jnp.float32
bf16 = jnp.bfloat16

GRID_W = 64
GLA_HEADS = 4
GLA_RANK = 16
GLA_TAU = 16.0
GLA_CHUNK = 32
ROPE_THETA = 10000.0
SSD_HEADDIM = 64
SSD_STATE = 128
SSD_GROUPS = 8
N_EXPERT_GROUPS = 8
TOPK_GROUPS = 4
TOP_K = 8
ROUTED_SCALE = 2.5
LN_EPS = 1e-5
RMS_EPS = 1e-6

LANES = 128
VMEM_BYTES_V7X = 64 * 2 ** 20


def _cparams(n_axes, vmem_mib):
    assert vmem_mib * 2 ** 20 < VMEM_BYTES_V7X
    return pltpu.CompilerParams(dimension_semantics=("arbitrary",) * n_axes,
                                vmem_limit_bytes=vmem_mib * 2 ** 20)


def _dot(a, b):
    return jnp.dot(a, b, preferred_element_type=f32)


def _dot_nt(a, b):
    return lax.dot_general(a, b, (((1,), (1,)), ((), ())), preferred_element_type=f32)


def _dot_tn(a, b):
    return lax.dot_general(a, b, (((0,), (0,)), ((), ())), preferred_element_type=f32)


def _split3(x):
    hi = x.astype(bf16)
    r = x - hi.astype(f32)
    mid = r.astype(bf16)
    lo = (r - mid.astype(f32)).astype(bf16)
    return hi, mid, lo


def _dot_exact_lhs(a_bf, x):
    hi, mid, lo = _split3(x)
    return (_dot(a_bf, lo) + _dot(a_bf, mid)) + _dot(a_bf, hi)


def _dot_exact_rhs(x, e_bf):
    hi, mid, lo = _split3(x)
    return (_dot(lo, e_bf) + _dot(mid, e_bf)) + _dot(hi, e_bf)


def _silu(x):
    return x * jax.nn.sigmoid(x)


def _cast_rows(src_ref, dst_ref, rows):
    n = src_ref.shape[0]
    rows = math.gcd(n, rows)

    def body(i, c):
        r = pl.multiple_of(i * rows, rows)
        dst_ref[pl.ds(r, rows), :] = src_ref[pl.ds(r, rows), :].astype(dst_ref.dtype)
        return c

    lax.fori_loop(0, n // rows, body, 0)


def _ada_body(c_ref, w_ref, b_ref, o_ref):
    a = _silu(c_ref[...]).astype(bf16)
    o_ref[0] = _dot(a, w_ref[0].astype(bf16)) + b_ref[0]


def ada_all(cond8, w_ada, b_ada, tn=1024):
    L, D, N = w_ada.shape
    return pl.pallas_call(
        _ada_body,
        grid=(L, N // tn),
        in_specs=[pl.BlockSpec((8, D), lambda l, n: (0, 0)),
                  pl.BlockSpec((1, D, tn), lambda l, n: (l, 0, n)),
                  pl.BlockSpec((1, 1, tn), lambda l, n: (l, 0, n))],
        out_specs=pl.BlockSpec((1, 8, tn), lambda l, n: (l, 0, n)),
        out_shape=jax.ShapeDtypeStruct((L, 8, N), f32),
        compiler_params=_cparams(2, 40),
        name="ada",
    )(cond8, w_ada, b_ada.reshape(L, 1, N))


N_ADA = 6


class Rows:
    def __init__(self, n_prompt, seq_s, n_seq_s):
        self.n_prompt = n_prompt
        self.seq_s = seq_s
        self.mods_per_layer = (1 + n_seq_s) * N_ADA

    def group(self, row0):
        return jnp.where(row0 < self.n_prompt, 0, 1 + (row0 - self.n_prompt) // self.seq_s)


def _mod_index(rows, layer, j, tm):
    assert rows.n_prompt % tm == 0 and rows.seq_s % tm == 0

    def index_map(i):
        return (layer * rows.mods_per_layer + rows.group(i * tm) * N_ADA + j, 0, 0)
    return index_map


def _mm_in_body(x_ref, sh_ref, sc_ref, w_ref, o_ref, wb_ref):
    @pl.when(pl.program_id(1) == 0)
    def _():
        _cast_rows(w_ref, wb_ref, 256)

    h = (x_ref[...] * (1.0 + sc_ref[0]) + sh_ref[0]).astype(bf16)
    o_ref[...] = _dot(h, wb_ref[...])


def mm_in(x, mods, rows, layer, w, col0, n_cols, tn, tm=512):
    nt, d = x.shape
    assert n_cols % tn == 0 and col0 % tn == 0 and nt % tm == 0
    sh = _mod_index(rows, layer, 0, tm)
    sc = _mod_index(rows, layer, 1, tm)
    cb = col0 // tn
    return pl.pallas_call(
        _mm_in_body,
        grid=(n_cols // tn, nt // tm),
        in_specs=[pl.BlockSpec((tm, d), lambda n, m: (m, 0)),
                  pl.BlockSpec((1, 1, d), lambda n, m: sh(m)),
                  pl.BlockSpec((1, 1, d), lambda n, m: sc(m)),
                  pl.BlockSpec((d, tn), lambda n, m: (0, cb + n))],
        out_specs=pl.BlockSpec((tm, tn), lambda n, m: (m, n)),
        out_shape=jax.ShapeDtypeStruct((nt, n_cols), f32),
        scratch_shapes=[pltpu.VMEM((d, tn), bf16)],
        compiler_params=_cparams(2, 48),
        name="mm_in",
    )(x, mods, mods, w)


def _layer_norm(v, g, b):
    mu = jnp.mean(v, axis=-1, keepdims=True)
    d = v - mu
    var = jnp.mean(d * d, axis=-1, keepdims=True)
    return d * lax.rsqrt(var + LN_EPS) * g + b


def _mm_post_body(y_ref, w_ref, x_ref, gate_ref, g_ref, b_ref, o_ref, acc_ref, *, nk, alpha):
    k = pl.program_id(1)
    part = _dot(y_ref[...], w_ref[...])

    @pl.when(k == 0)
    def _():
        acc_ref[...] = part

    @pl.when(k > 0)
    def _():
        acc_ref[...] += part

    @pl.when(k == nk - 1)
    def _():
        v = alpha * x_ref[...] + gate_ref[0] * acc_ref[...]
        o_ref[...] = _layer_norm(v, g_ref[...], b_ref[...])


def mm_post(y, w_bf, x, mods, rows, layer, gate_j, ln_g, ln_b, alpha, tm=256, tk=2048):
    nt, kdim = y.shape
    d = x.shape[1]
    nk = kdim // tk
    gate = _mod_index(rows, layer, gate_j, tm)
    return pl.pallas_call(
        functools.partial(_mm_post_body, nk=nk, alpha=alpha),
        grid=(nt // tm, nk),
        in_specs=[pl.BlockSpec((tm, tk), lambda m, k: (m, k)),
                  pl.BlockSpec((tk, d), lambda m, k: (k, 0)),
                  pl.BlockSpec((tm, d), lambda m, k: (m, 0)),
                  pl.BlockSpec((1, 1, d), lambda m, k: gate(m)),
                  pl.BlockSpec((1, d), lambda m, k: (0, 0)),
                  pl.BlockSpec((1, d), lambda m, k: (0, 0))],
        out_specs=pl.BlockSpec((tm, d), lambda m, k: (m, 0)),
        out_shape=jax.ShapeDtypeStruct((nt, d), f32),
        scratch_shapes=[pltpu.VMEM((tm, d), f32)],
        compiler_params=_cparams(2, 48),
        name="mm_post",
    )(y, w_bf, x, mods, ln_g.reshape(1, d), ln_b.reshape(1, d))


def _chunk_masks(tb, c, reverse):
    i = lax.broadcasted_iota(jnp.int32, (tb, tb), 0)
    j = lax.broadcasted_iota(jnp.int32, (tb, tb), 1)
    same = (i // c) == (j // c)
    tri = (j >= i) if reverse else (j <= i)
    return jnp.logical_and(same, tri), same


def _gla_direction(q, k, v, g, wg, bg, rope, st_ref, reverse, c):
    tb, dk = q.shape
    assert tb % (2 * c) == 0
    causal, same = _chunk_masks(tb, c, reverse)
    causal_bf = jnp.where(causal, 1.0, 0.0).astype(bf16)
    same_bf = jnp.where(same, 1.0, 0.0).astype(bf16)

    pre = _dot(g.astype(bf16), wg.astype(bf16)) + bg
    log_a = (jnp.minimum(pre, 0.0) - jnp.log1p(jnp.exp(-jnp.abs(pre)))) * (1.0 / GLA_TAU)
    b = _dot_exact_lhs(causal_bf, log_a)
    total = _dot_exact_lhs(same_bf, log_a)

    q = q * (dk ** -0.5)
    if rope is not None:
        cos, sin = rope
        half = dk // 2
        q = jnp.concatenate([q[:, :half] * cos - q[:, half:] * sin, q[:, :half] * sin + q[:, half:] * cos], axis=1)
        k = jnp.concatenate([k[:, :half] * cos - k[:, half:] * sin, k[:, :half] * sin + k[:, half:] * cos], axis=1)

    q_loc = q * jnp.exp(b)
    k_loc = k * jnp.exp(total - b)
    q_dec = q_loc.astype(bf16)
    k_inv = (k * jnp.exp(-b)).astype(bf16)
    k_end = k_loc.astype(bf16)
    v_bf = v.astype(bf16)

    pair = 2 * c
    n_pairs = tb // pair
    sub = lax.broadcasted_iota(jnp.int32, (tb, dk), 0) // c
    scanned_first = (sub % 2) == (1 if reverse else 0)
    total_other = jnp.where(sub % 2 == 0, pltpu.roll(total, tb - c, 0), pltpu.roll(total, c, 0))
    e_other = jnp.exp(total_other)
    q_pair = (q_loc * jnp.where(scanned_first, 1.0, e_other)).astype(bf16)
    k_pair = (k_loc * jnp.where(scanned_first, e_other, 1.0)).astype(bf16)
    dec_pair = jnp.exp(total + total_other)

    i = lax.broadcasted_iota(jnp.int32, (tb, tb), 0)
    j = lax.broadcasted_iota(jnp.int32, (tb, tb), 1)
    i_second = ((i // c) % 2) == (0 if reverse else 1)
    cross = jnp.logical_and(jnp.logical_and((i // pair) == (j // pair), (i // c) != (j // c)), i_second)
    scores = jnp.where(causal, _dot_nt(q_dec, k_inv), 0.0) + jnp.where(cross, _dot_nt(q_dec, k_end), 0.0)
    o_intra = _dot(scores.astype(bf16), v_bf)

    order = range(n_pairs - 1, -1, -1) if reverse else range(n_pairs)
    o_inter = [None] * n_pairs
    st = st_ref[...]
    for pi in order:
        sl = slice(pi * pair, (pi + 1) * pair)
        o_inter[pi] = _dot_nt(q_pair[sl], st.astype(bf16))
        st = st * dec_pair[pi * pair:pi * pair + 1, :] + _dot_tn(v_bf[sl], k_pair[sl])
    st_ref[...] = st
    return o_intra + jnp.concatenate(o_inter, axis=0)


def _gla_body(*refs, c, use_rope, has_init, emit_state, nb):
    refs = list(refs)
    qf, kf, vf, gf, qb, kb, vb, gb, wgf, wgb, bgf, bgb = refs[:12]
    refs = refs[12:]
    rope_f = rope_b = None
    if use_rope:
        rope_f = (refs[0][...], refs[1][...])
        rope_b = (refs[2][...], refs[3][...])
        refs = refs[4:]
    if has_init:
        s0_ref = refs[0]
        refs = refs[1:]
    of_ref, ob_ref = refs[:2]
    refs = refs[2:]
    if emit_state:
        sfin_ref = refs[0]
        refs = refs[1:]
    stf_ref, stb_ref = refs
    t = pl.program_id(2)

    @pl.when(t == 0)
    def _():
        if has_init:
            stf_ref[...] = s0_ref[0, 0, 0].T
            stb_ref[...] = s0_ref[0, 1, 0].T
        else:
            stf_ref[...] = jnp.zeros_like(stf_ref)
            stb_ref[...] = jnp.zeros_like(stb_ref)

    of_ref[...] = _gla_direction(qf[...], kf[...], vf[...], gf[...], wgf[...], bgf[0], rope_f, stf_ref, False, c)
    ob_ref[...] = _gla_direction(qb[...], kb[...], vb[...], gb[...], wgb[...], bgb[0], rope_b, stb_ref, True, c)

    if emit_state:
        @pl.when(t == nb - 1)
        def _():
            sfin_ref[0, 0, 0] = stf_ref[...].T
            sfin_ref[0, 1, 0] = stb_ref[...].T


def gla_scan(proj, g, wg_pad, b_gate, row0, n_seq, seq_len, rope=None, s0=None, emit_state=False, tb=256):
    h = GLA_HEADS
    hk = wg_pad.shape[2]
    dk = hk // h
    dv = 2 * dk
    nb = seq_len // tb
    base = row0 // tb
    assert seq_len % tb == 0 and row0 % tb == 0 and dv == 2 * dk

    def fblk(s, t):
        return base + s * nb + t

    def bblk(s, t):
        return base + s * nb + (nb - 1 - t)

    v_off = 2 * hk // dv
    in_specs, args = [], []
    for blk in (fblk, bblk):
        in_specs += [pl.BlockSpec((tb, dk), lambda s, hh, t, blk=blk: (blk(s, t), hh)),
                     pl.BlockSpec((tb, dk), lambda s, hh, t, blk=blk: (blk(s, t), h + hh)),
                     pl.BlockSpec((tb, dv), lambda s, hh, t, blk=blk: (blk(s, t), v_off + hh)),
                     pl.BlockSpec((tb, LANES), lambda s, hh, t, blk=blk: (blk(s, t), 0))]
        args += [proj, proj, proj, g]
    in_specs += [pl.BlockSpec((None, LANES, dk), lambda s, hh, t: (0, 0, hh)),
                 pl.BlockSpec((None, LANES, dk), lambda s, hh, t: (1, 0, hh)),
                 pl.BlockSpec((None, 1, dk), lambda s, hh, t: (0, 0, hh)),
                 pl.BlockSpec((None, 1, dk), lambda s, hh, t: (1, 0, hh))]
    bg3 = b_gate.reshape(2, 1, hk)
    args += [wg_pad, wg_pad, bg3, bg3]
    if rope is not None:
        for tab_blk in (lambda s, hh, t: (t, 0), lambda s, hh, t: (nb - 1 - t, 0)):
            in_specs += [pl.BlockSpec((tb, dk // 2), tab_blk), pl.BlockSpec((tb, dk // 2), tab_blk)]
            args += [rope[0], rope[1]]
    if s0 is not None:
        in_specs.append(pl.BlockSpec((1, 2, 1, dk, dv), lambda s, hh, t: (s, 0, hh, 0, 0)))
        args.append(s0)
    n_rows = n_seq * seq_len
    out_specs = [pl.BlockSpec((tb, dv), lambda s, hh, t: (s * nb + t, hh)),
                 pl.BlockSpec((tb, dv), lambda s, hh, t: (s * nb + (nb - 1 - t), hh))]
    out_shape = [jax.ShapeDtypeStruct((n_rows, h * dv), f32)] * 2
    if emit_state:
        out_specs.append(pl.BlockSpec((1, 2, 1, dk, dv), lambda s, hh, t: (s, 0, hh, 0, 0)))
        out_shape.append(jax.ShapeDtypeStruct((n_seq, 2, h, dk, dv), f32))
    return pl.pallas_call(
        functools.partial(_gla_body, c=GLA_CHUNK, use_rope=rope is not None, has_init=s0 is not None,
                          emit_state=emit_state, nb=nb),
        grid=(n_seq, h, nb),
        in_specs=in_specs,
        out_specs=out_specs,
        out_shape=out_shape,
        scratch_shapes=[pltpu.VMEM((dv, dk), f32), pltpu.VMEM((dv, dk), f32)],
        compiler_params=_cparams(3, 40),
        name="gla_scan",
    )(*args)


def _gla_gate_body(of_ref, ob_ref, r_ref, g_ref, y_ref):
    o = of_ref[...] + ob_ref[...]
    o = o * lax.rsqrt(jnp.mean(o * o, axis=-1, keepdims=True) + RMS_EPS) * g_ref[...]
    y_ref[...] = (o * _silu(r_ref[...])).astype(y_ref.dtype)


def gla_gate(o_f, o_b, proj, norm_g, r_col0, row0, tm=512):
    nt, hv = o_f.shape
    dv = norm_g.shape[0]
    r_off = r_col0 // dv
    rb = row0 // tm
    assert row0 % tm == 0 and nt % tm == 0
    return pl.pallas_call(
        _gla_gate_body,
        grid=(nt // tm, hv // dv),
        in_specs=[pl.BlockSpec((tm, dv), lambda m, hh: (m, hh)),
                  pl.BlockSpec((tm, dv), lambda m, hh: (m, hh)),
                  pl.BlockSpec((tm, dv), lambda m, hh: (rb + m, r_off + hh)),
                  pl.BlockSpec((1, dv), lambda m, hh: (0, 0))],
        out_specs=pl.BlockSpec((tm, dv), lambda m, hh: (m, hh)),
        out_shape=jax.ShapeDtypeStruct((nt, hv), bf16),
        compiler_params=_cparams(2, 32),
        name="gla_gate",
    )(o_f, o_b, proj, norm_g.reshape(1, dv))


def _conv_body(x_ref, w_ref, b_ref, o_ref, *, seq_len):
    x = x_ref[...]
    t = x.shape[0]
    pos = lax.broadcasted_iota(jnp.int32, x.shape, 0) % seq_len
    prev = jnp.where(pos == 0, 0.0, pltpu.roll(x, 1, 0))
    nxt = jnp.where(pos == seq_len - 1, 0.0, pltpu.roll(x, t - 1, 0))
    w = w_ref[...]
    o_ref[...] = _silu(prev * w[0:1] + x * w[1:2] + nxt * w[2:3] + b_ref[...])


CONV_BLOCK_ROWS = 4096


def ssd_conv(proj, conv_w, conv_b, col0, row0, n_seq, seq_len, tc=4 * LANES):
    nc = conv_w.shape[1]
    n_rows = n_seq * seq_len
    tr = max(seq_len, math.gcd(n_rows, CONV_BLOCK_ROWS))
    assert col0 % tc == 0 and nc % tc == 0 and tr % seq_len == 0 and n_rows % tr == 0 and row0 % tr == 0
    cb, rb = col0 // tc, row0 // tr
    return pl.pallas_call(
        functools.partial(_conv_body, seq_len=seq_len),
        grid=(n_rows // tr, nc // tc),
        in_specs=[pl.BlockSpec((tr, tc), lambda s, ci: (rb + s, cb + ci)),
                  pl.BlockSpec((3, tc), lambda s, ci: (0, ci)),
                  pl.BlockSpec((1, tc), lambda s, ci: (0, ci))],
        out_specs=pl.BlockSpec((tr, tc), lambda s, ci: (s, ci)),
        out_shape=jax.ShapeDtypeStruct((n_seq * seq_len, nc), f32),
        compiler_params=_cparams(2, 48),
        name="ssd_conv",
    )(proj, conv_w, conv_b.reshape(1, nc))


def _softplus(x):
    return jnp.maximum(x, 0.0) + jnp.log1p(jnp.exp(-jnp.abs(x)))


def _ssd_direction(x, bm, cm, dt_raw, bias, a_row, e_wide, e_head, ht_ref, reverse):
    tb, width = x.shape
    p = SSD_HEADDIM
    n_heads = width // p
    i = lax.broadcasted_iota(jnp.int32, (tb, tb), 0)
    j = lax.broadcasted_iota(jnp.int32, (tb, tb), 1)
    causal = (j >= i) if reverse else (j <= i)
    causal_bf = jnp.where(causal, 1.0, 0.0).astype(bf16)

    dt = _softplus(dt_raw + bias)
    cs = _dot_exact_lhs(causal_bf, dt * a_row)
    cs_w = _dot_exact_rhs(cs, e_wide)
    dt_w = _dot_exact_rhs(dt, e_wide)
    cs_h = _dot_exact_rhs(cs, e_head)
    cs_ht = cs_h.T
    last = 0 if reverse else tb - 1
    cs_last = cs_w[last:last + 1, :]

    bm_bf = bm.astype(bf16)
    cm_bf = cm.astype(bf16)
    cb = _dot_nt(cm_bf, bm_bf)
    xdt = (x * dt_w).astype(bf16)
    lane = lax.broadcasted_iota(jnp.int32, (tb, LANES), 1)
    heads_per_slab = LANES // p
    y_slabs = []
    for sidx in range(width // LANES):
        xs = xdt[:, sidx * LANES:(sidx + 1) * LANES]
        acc = None
        for hs in range(heads_per_slab):
            hh = sidx * heads_per_slab + hs
            diff = cs_h[:, hh:hh + 1] - cs_ht[hh:hh + 1, :]
            m = (cb * jnp.exp(jnp.where(causal, diff, -jnp.inf))).astype(bf16)
            part = _dot(m, jnp.where((lane // p) == hs, xs, jnp.zeros_like(xs)))
            acc = part if acc is None else acc + part
        y_slabs.append(acc)
    ht = ht_ref[...]
    y = jnp.concatenate(y_slabs, axis=1) + _dot(cm_bf, ht.astype(bf16)) * jnp.exp(cs_w)
    xw = (x * (jnp.exp(cs_last - cs_w) * dt_w)).astype(bf16)
    ht_ref[...] = ht * jnp.exp(cs_last) + _dot_tn(bm_bf, xw)
    return y


def _ssd_body(*refs, has_init, emit_state, nb):
    refs = list(refs)
    xf, bf_, cf, dtf, xb, bb, cb_, dtb, bias, alog, ewf, ewb, ehf, ehb = refs[:14]
    refs = refs[14:]
    if has_init:
        s0_ref = refs[0]
        refs = refs[1:]
    yf_ref, yb_ref = refs[:2]
    refs = refs[2:]
    if emit_state:
        sfin_ref = refs[0]
        refs = refs[1:]
    htf_ref, htb_ref = refs
    t = pl.program_id(2)
    n, width = htf_ref.shape

    @pl.when(t == 0)
    def _():
        if has_init:
            htf_ref[...] = s0_ref[0, 0].reshape(width, n).T
            htb_ref[...] = s0_ref[0, 1].reshape(width, n).T
        else:
            htf_ref[...] = jnp.zeros_like(htf_ref)
            htb_ref[...] = jnp.zeros_like(htb_ref)

    a_row = -jnp.exp(alog[...])
    yf_ref[...] = _ssd_direction(xf[...], bf_[...], cf[...], dtf[...], bias[...], a_row, ewf[...], ehf[...],
                                 htf_ref, False)
    yb_ref[...] = _ssd_direction(xb[...], bb[...], cb_[...], dtb[...], bias[...], a_row, ewb[...], ehb[...],
                                 htb_ref, True)

    if emit_state:
        @pl.when(t == nb - 1)
        def _():
            sfin_ref[0, 0] = htf_ref[...].T.reshape(sfin_ref.shape[2:])
            sfin_ref[0, 1] = htb_ref[...].T.reshape(sfin_ref.shape[2:])


def _ssd_select_matrices(n_heads):
    hg = n_heads // SSD_GROUPS
    col = jnp.arange(2 * n_heads)[None, None, :, None]
    d = jnp.arange(2)[:, None, None, None]
    g = jnp.arange(SSD_GROUPS)[None, :, None, None]
    wide = jnp.arange(hg * SSD_HEADDIM)[None, None, None, :] // SSD_HEADDIM
    head = jnp.arange(LANES)[None, None, None, :]
    first = d * n_heads + g * hg
    e_wide = (col == first + wide).astype(bf16)
    e_head = jnp.logical_and(col == first + head, head < hg).astype(bf16)
    return e_wide, e_head


def ssd_scan(xbc, dt, dt_bias, a_log, row0, n_seq, seq_len, s0=None, emit_state=False, tb=256):
    n_heads = dt_bias.shape[1]
    assert 2 * n_heads == LANES
    hg = n_heads // SSD_GROUPS
    p, n = SSD_HEADDIM, SSD_STATE
    width = hg * p
    di = n_heads * p
    nb = seq_len // tb
    base = row0 // tb
    assert seq_len % tb == 0 and row0 % tb == 0 and n == LANES
    e_wide, e_head = _ssd_select_matrices(n_heads)
    b_off = di // n
    c_off = b_off + SSD_GROUPS

    in_specs, args = [], []
    for rev in (False, True):
        def blk(s, t, rev=rev):
            return s * nb + ((nb - 1 - t) if rev else t)
        in_specs += [pl.BlockSpec((tb, width), lambda s, g, t, blk=blk: (blk(s, t), g)),
                     pl.BlockSpec((tb, n), lambda s, g, t, blk=blk: (blk(s, t), b_off + g)),
                     pl.BlockSpec((tb, n), lambda s, g, t, blk=blk: (blk(s, t), c_off + g)),
                     pl.BlockSpec((tb, LANES), lambda s, g, t, blk=blk: (base + blk(s, t), 0))]
        args += [xbc, xbc, xbc, dt]
    in_specs += [pl.BlockSpec((1, LANES), lambda s, g, t: (0, 0)),
                 pl.BlockSpec((1, LANES), lambda s, g, t: (0, 0)),
                 pl.BlockSpec((None, None, LANES, width), lambda s, g, t: (0, g, 0, 0)),
                 pl.BlockSpec((None, None, LANES, width), lambda s, g, t: (1, g, 0, 0)),
                 pl.BlockSpec((None, None, LANES, LANES), lambda s, g, t: (0, g, 0, 0)),
                 pl.BlockSpec((None, None, LANES, LANES), lambda s, g, t: (1, g, 0, 0))]
    args += [dt_bias.reshape(1, LANES), a_log.reshape(1, LANES), e_wide, e_wide, e_head, e_head]
    if s0 is not None:
        in_specs.append(pl.BlockSpec((1, 2, hg, p, n), lambda s, g, t: (s, 0, g, 0, 0)))
        args.append(s0)
    n_rows = n_seq * seq_len
    out_specs = [pl.BlockSpec((tb, width), lambda s, g, t: (s * nb + t, g)),
                 pl.BlockSpec((tb, width), lambda s, g, t: (s * nb + (nb - 1 - t), g))]
    out_shape = [jax.ShapeDtypeStruct((n_rows, di), f32)] * 2
    if emit_state:
        out_specs.append(pl.BlockSpec((1, 2, hg, p, n), lambda s, g, t: (s, 0, g, 0, 0)))
        out_shape.append(jax.ShapeDtypeStruct((n_seq, 2, n_heads, p, n), f32))
    return pl.pallas_call(
        functools.partial(_ssd_body, has_init=s0 is not None, emit_state=emit_state, nb=nb),
        grid=(n_seq, SSD_GROUPS, nb),
        in_specs=in_specs,
        out_specs=out_specs,
        out_shape=out_shape,
        scratch_shapes=[pltpu.VMEM((n, width), f32), pltpu.VMEM((n, width), f32)],
        compiler_params=_cparams(3, 40),
        name="ssd_scan",
    )(*args)


def _ssd_gate_body(yf_ref, yb_ref, xs_ref, z_ref, d_ref, g_ref, o_ref):
    y = yf_ref[...] + yb_ref[...] + d_ref[...] * xs_ref[...]
    y = y * _silu(z_ref[...])
    y = y * lax.rsqrt(jnp.mean(y * y, axis=-1, keepdims=True) + RMS_EPS)
    o_ref[...] = (y * g_ref[...]).astype(o_ref.dtype)


def ssd_gate(y_f, y_b, xbc, proj, d_rep, norm_g, row0, tm=512):
    nt, di = y_f.shape
    gw = di // SSD_GROUPS
    rb = row0 // tm
    assert row0 % tm == 0 and nt % tm == 0
    return pl.pallas_call(
        _ssd_gate_body,
        grid=(nt // tm, SSD_GROUPS),
        in_specs=[pl.BlockSpec((tm, gw), lambda m, gg: (m, gg)),
                  pl.BlockSpec((tm, gw), lambda m, gg: (m, gg)),
                  pl.BlockSpec((tm, gw), lambda m, gg: (m, gg)),
                  pl.BlockSpec((tm, gw), lambda m, gg: (rb + m, gg)),
                  pl.BlockSpec((1, gw), lambda m, gg: (0, gg)),
                  pl.BlockSpec((1, gw), lambda m, gg: (0, gg))],
        out_specs=pl.BlockSpec((tm, gw), lambda m, gg: (m, gg)),
        out_shape=jax.ShapeDtypeStruct((nt, di), bf16),
        compiler_params=_cparams(2, 32),
        name="ssd_gate",
    )(y_f, y_b, xbc, proj, d_rep.reshape(1, di), norm_g.reshape(1, di))


def _to_slabs(v, slab_ref):
    rows, d = v.shape
    n_slab = d // LANES
    for s in range(n_slab):
        slab_ref[pl.ds(s, rows, stride=n_slab), :] = v[:, s * LANES:(s + 1) * LANES]


def _from_slabs(slab_ref, rows):
    n_slab = slab_ref.shape[0] // rows
    return jnp.concatenate([slab_ref[pl.ds(s, rows, stride=n_slab), :] for s in range(n_slab)], axis=1)


def _router_body(x_ref, sh_ref, sc_ref, w_ref, b_ref, h_ref, idx_ref, wt_ref, *, n_exp):
    h = x_ref[...] * (1.0 + sc_ref[0]) + sh_ref[0]
    _to_slabs(h, h_ref)
    tm = h.shape[0]
    scores = jax.nn.sigmoid(_dot(h.astype(bf16), w_ref[...].astype(bf16)))
    choice = scores + b_ref[...]
    n_grp = N_EXPERT_GROUPS
    per_grp = n_exp // n_grp
    s3 = scores.T[:n_exp].reshape(n_grp, per_grp, tm)
    c3 = choice.T[:n_exp].reshape(n_grp, per_grp, tm)
    ninf = -jnp.inf
    pos = lax.broadcasted_iota(jnp.int32, c3.shape, 1)
    grp = lax.broadcasted_iota(jnp.int32, c3.shape, 0)
    eidx = grp * per_grp + pos

    m1 = jnp.max(c3, axis=1, keepdims=True)
    i1 = jnp.min(jnp.where(c3 == m1, pos, per_grp), axis=1, keepdims=True)
    m2 = jnp.max(jnp.where(pos == i1, ninf, c3), axis=1, keepdims=True)
    gscore = m1 + m2
    gpos = lax.broadcasted_iota(jnp.int32, gscore.shape, 0)
    keep = jnp.zeros(gscore.shape, jnp.bool_)
    for _ in range(TOPK_GROUPS):
        m = jnp.max(gscore, axis=0, keepdims=True)
        gi = jnp.min(jnp.where(gscore == m, gpos, n_grp), axis=0, keepdims=True)
        hit = gpos == gi
        keep = jnp.logical_or(keep, hit)
        gscore = jnp.where(hit, ninf, gscore)

    cur = jnp.where(keep, c3, ninf)
    picked_w = []
    for k in range(TOP_K):
        m = jnp.max(jnp.max(cur, axis=1, keepdims=True), axis=0, keepdims=True)
        ei = jnp.min(jnp.min(jnp.where(cur == m, eidx, n_exp), axis=1, keepdims=True), axis=0, keepdims=True)
        hit = eidx == ei
        wk = jnp.sum(jnp.sum(jnp.where(hit, s3, 0.0), axis=1, keepdims=True), axis=0, keepdims=True)
        cur = jnp.where(hit, ninf, cur)
        idx_ref[k:k + 1, :] = ei.reshape(1, tm)
        picked_w.append(wk.reshape(1, tm))
    total = picked_w[0]
    for wk in picked_w[1:]:
        total = total + wk
    for k in range(TOP_K):
        wt_ref[k:k + 1, :] = picked_w[k] / total * ROUTED_SCALE


def moe_router(x, mods, rows, layer, w_router, router_bias, tm=512):
    nt, d = x.shape
    n_exp = w_router.shape[1]
    n_slab = d // LANES
    assert n_exp <= LANES and n_exp % N_EXPERT_GROUPS == 0 and d % LANES == 0
    w_pad = jnp.pad(w_router, ((0, 0), (0, LANES - n_exp)))
    b_pad = jnp.pad(router_bias, (0, LANES - n_exp)).reshape(1, LANES)
    sh = _mod_index(rows, layer, 3, tm)
    sc = _mod_index(rows, layer, 4, tm)
    return pl.pallas_call(
        functools.partial(_router_body, n_exp=n_exp),
        grid=(nt // tm,),
        in_specs=[pl.BlockSpec((tm, d), lambda m: (m, 0)),
                  pl.BlockSpec((1, 1, d), sh),
                  pl.BlockSpec((1, 1, d), sc),
                  pl.BlockSpec((d, LANES), lambda m: (0, 0)),
                  pl.BlockSpec((1, LANES), lambda m: (0, 0))],
        out_specs=[pl.BlockSpec((tm * n_slab, LANES), lambda m: (m, 0)),
                   pl.BlockSpec((TOP_K, tm), lambda m: (0, m)),
                   pl.BlockSpec((TOP_K, tm), lambda m: (0, m))],
        out_shape=[jax.ShapeDtypeStruct((nt * n_slab, LANES), f32),
                   jax.ShapeDtypeStruct((TOP_K, nt), jnp.int32),
                   jax.ShapeDtypeStruct((TOP_K, nt), f32)],
        compiler_params=_cparams(1, 32),
        name="moe_router",
    )(x, mods, mods, w_pad, b_pad)


def routing_tables(idx_t, n_exp, tmx):
    nt = idx_t.shape[1]
    tk = TOP_K * nt
    e_flat = idx_t.reshape(tk)
    _, order = lax.sort_key_val(e_flat, jnp.arange(tk, dtype=jnp.int32))
    experts = jnp.arange(n_exp, dtype=jnp.int32)
    counts = jnp.sum((e_flat[None, :] == experts[:, None]).astype(jnp.int32), axis=1)
    start = jnp.cumsum(counts) - counts
    n_blk = (counts + tmx - 1) // tmx
    blk_end = jnp.cumsum(n_blk)
    n_blocks = -(-(tk + n_exp * (tmx - 1)) // tmx)
    b = jnp.arange(n_blocks, dtype=jnp.int32)
    blk_expert = jnp.minimum(jnp.sum((blk_end[None, :] <= b[:, None]).astype(jnp.int32), axis=1), n_exp - 1)
    onehot = (blk_expert[:, None] == experts[None, :]).astype(jnp.int32)
    pick = lambda v: jnp.sum(onehot * v[None, :], axis=1)
    offset = (b - pick(blk_end - n_blk)) * tmx
    blk_rows = jnp.clip(pick(counts) - offset, 0, tmx)
    blk_src = jnp.clip(pick(start) + offset, 0, tk - 1)
    return order, blk_expert, blk_src, blk_rows


def _slab_copy(src_ref, src_slab, dst_ref, dst_slab, n_slab, sem):
    src = src_ref.at[pl.ds(pl.multiple_of(src_slab * n_slab, n_slab), n_slab), :]
    dst = dst_ref.at[pl.ds(pl.multiple_of(dst_slab * n_slab, n_slab), n_slab), :]
    return pltpu.make_async_copy(src, dst, sem)


def _expert_body(order_ref, be_ref, src_ref, rows_ref, h_hbm, wg_ref, wu_ref, wd_ref, y_hbm,
                 xbuf, ybuf, xs_ref, wgb, wub, wdb, gsem, ssem, *, tmx, nt, n_slab):
    i = pl.program_id(0)
    n = pl.num_programs(0)
    tk = order_ref.shape[0]
    slot = i % 2
    used = rows_ref[i] > 0

    def gather_copy(base, r, s):
        a = order_ref[jnp.minimum(base + r, tk - 1)]
        return _slab_copy(h_hbm, lax.rem(a, nt), xbuf.at[s], r, n_slab, gsem.at[s])

    def scatter_copy(base, n_real, r, s):
        a = jnp.where(r < n_real, order_ref[jnp.minimum(base + r, tk - 1)], tk + s * tmx + r)
        return _slab_copy(ybuf.at[s], r, y_hbm, a, n_slab, ssem.at[s])

    def wait_gather(s):
        pltpu.make_async_copy(h_hbm.at[pl.ds(0, tmx * n_slab), :], xbuf.at[s], gsem.at[s]).wait()

    def wait_scatter(s):
        pltpu.make_async_copy(ybuf.at[s], y_hbm.at[pl.ds(0, tmx * n_slab), :], ssem.at[s]).wait()

    @pl.when(i == 0)
    def _():
        ybuf[...] = jnp.zeros_like(ybuf)
        for s in range(2):
            spare = y_hbm.at[pl.ds((tk + s * tmx) * n_slab, tmx * n_slab), :]
            fill = pltpu.make_async_copy(ybuf.at[s], spare, ssem.at[s])
            fill.start()
            fill.wait()

        def issue(r, c):
            gather_copy(src_ref[0], r, 0).start()
            return c

        lax.fori_loop(0, tmx, issue, 0, unroll=8)

    first_of_expert = jnp.logical_or(i == 0, be_ref[i] != be_ref[jnp.maximum(i - 1, 0)])

    @pl.when(jnp.logical_and(used, first_of_expert))
    def _():
        _cast_rows(wg_ref, wgb, 256)
        _cast_rows(wu_ref, wub, 256)
        _cast_rows(wd_ref, wdb, 256)

    has_next = jnp.logical_and(i + 1 < n, rows_ref[jnp.minimum(i + 1, n - 1)] > 0)
    nxt = jnp.where(has_next, i + 1, i)
    prev = jnp.maximum(i - 1, 0)
    prev_rows = jnp.where(i > 0, rows_ref[prev], 0)

    @pl.when(used)
    def _():
        wait_gather(slot)
        xs_ref[...] = _from_slabs(xbuf.at[slot], tmx).astype(bf16)
        nxt_base = src_ref[nxt]
        prev_base = src_ref[prev]
        for r in range(tmx):
            gather_copy(nxt_base, r, 1 - slot).start(priority=1)
            scatter_copy(prev_base, prev_rows, r, 1 - slot).start(priority=0)
        x = xs_ref[...]
        hmid = (_silu(_dot(x, wgb[...])) * _dot(x, wub[...])).astype(bf16)
        _to_slabs(_dot(hmid, wdb[...]), ybuf.at[slot])
        wait_scatter(1 - slot)

    @pl.when(jnp.logical_and(used, jnp.logical_not(has_next)))
    def _():
        wait_gather(1 - slot)
        base, n_real = src_ref[i], rows_ref[i]

        def issue(r, c):
            scatter_copy(base, n_real, r, slot).start()
            return c

        lax.fori_loop(0, tmx, issue, 0, unroll=8)
        wait_scatter(slot)


def moe_experts(h_slabs, order, blk_expert, blk_src, blk_rows, w_gate, w_up, w_down, layer, tmx):
    n_blocks = blk_expert.shape[0]
    tk = order.shape[0]
    d, ff = w_gate.shape[2:]
    n_slab = d // LANES
    nt = h_slabs.shape[0] // n_slab
    wmap = lambda i, order, be, src, rows: (layer, be[i], 0, 0)
    grid_spec = pltpu.PrefetchScalarGridSpec(
        num_scalar_prefetch=4,
        grid=(n_blocks,),
        in_specs=[pl.BlockSpec(memory_space=pl.ANY),
                  pl.BlockSpec((None, None, d, ff), wmap),
                  pl.BlockSpec((None, None, d, ff), wmap),
                  pl.BlockSpec((None, None, ff, d), wmap)],
        out_specs=pl.BlockSpec(memory_space=pl.ANY),
        scratch_shapes=[pltpu.VMEM((2, tmx * n_slab, LANES), f32),
                        pltpu.VMEM((2, tmx * n_slab, LANES), f32),
                        pltpu.VMEM((tmx, d), bf16),
                        pltpu.VMEM((d, ff), bf16), pltpu.VMEM((d, ff), bf16), pltpu.VMEM((ff, d), bf16),
                        pltpu.SemaphoreType.DMA((2,)), pltpu.SemaphoreType.DMA((2,))],
    )
    return pl.pallas_call(
        functools.partial(_expert_body, tmx=tmx, nt=nt, n_slab=n_slab),
        grid_spec=grid_spec,
        out_shape=jax.ShapeDtypeStruct(((tk + 2 * tmx) * n_slab, LANES), f32),
        compiler_params=_cparams(1, 48),
        name="moe_experts",
    )(order, blk_expert, blk_src, blk_rows, h_slabs, w_gate, w_up, w_down)


def _ffn_post_body(*refs, alpha):
    h_ref = refs[0]
    y_refs = refs[1:1 + TOP_K]
    wrep_ref, sg_ref, su_ref, sd_ref, x_ref, gate_ref, g_ref, b_ref, o_ref, slab_scr = refs[1 + TOP_K:]
    tm = x_ref.shape[0]
    hb = _from_slabs(h_ref, tm).astype(bf16)
    hmid = (_silu(_dot(hb, sg_ref[...])) * _dot(hb, su_ref[...])).astype(bf16)
    ffn = _dot(hmid, sd_ref[...])
    wrep = wrep_ref[...]
    acc = None
    for k in range(TOP_K):
        part = y_refs[k][...] * wrep[:, k:k + 1]
        acc = part if acc is None else acc + part
    slab_scr[...] = acc
    v = alpha * x_ref[...] + gate_ref[0] * (_from_slabs(slab_scr, tm) + ffn)
    o_ref[...] = _layer_norm(v, g_ref[...], b_ref[...])


def moe_ffn_post(h_slabs, x, y_slabs, w_t, sg_bf, su_bf, sd_bf, mods, rows, layer, ln_g, ln_b, alpha, tm=128):
    nt, d = x.shape
    ff = sg_bf.shape[1]
    n_slab = d // LANES
    n_tiles = nt // tm
    w_rep = jnp.repeat(w_t.T, n_slab, axis=0)
    gate = _mod_index(rows, layer, 5, tm)
    slab_block = (tm * n_slab, LANES)
    y_specs = [pl.BlockSpec(slab_block, lambda m, k=k: (k * n_tiles + m, 0)) for k in range(TOP_K)]
    return pl.pallas_call(
        functools.partial(_ffn_post_body, alpha=alpha),
        grid=(n_tiles,),
        in_specs=[pl.BlockSpec(slab_block, lambda m: (m, 0))] + y_specs + [
                  pl.BlockSpec((tm * n_slab, TOP_K), lambda m: (m, 0)),
                  pl.BlockSpec((d, ff), lambda m: (0, 0)),
                  pl.BlockSpec((d, ff), lambda m: (0, 0)),
                  pl.BlockSpec((ff, d), lambda m: (0, 0)),
                  pl.BlockSpec((tm, d), lambda m: (m, 0)),
                  pl.BlockSpec((1, 1, d), gate),
                  pl.BlockSpec((1, d), lambda m: (0, 0)),
                  pl.BlockSpec((1, d), lambda m: (0, 0))],
        out_specs=pl.BlockSpec((tm, d), lambda m: (m, 0)),
        out_shape=jax.ShapeDtypeStruct((nt, d), f32),
        scratch_shapes=[pltpu.VMEM(slab_block, f32)],
        compiler_params=_cparams(1, 48),
        name="moe_ffn_post",
    )(h_slabs, *([y_slabs] * TOP_K), w_rep, sg_bf, su_bf, sd_bf, x, mods, ln_g.reshape(1, d), ln_b.reshape(1, d))


MOE_ROWS_PER_BLOCK = 256


def _grid_rope(seq_len, dk):
    t = jnp.arange(seq_len)
    row = (t // GRID_W).astype(f32)
    col = (t % GRID_W).astype(f32)
    nf = dk // 4
    freqs = ROPE_THETA ** (-jnp.arange(nf, dtype=f32) / nf)
    ang = jnp.concatenate([row[:, None] * freqs, col[:, None] * freqs], axis=-1)
    return jnp.cos(ang), jnp.sin(ang)


def kernel(x_prompt, x_sample, c, state_gla, state_ssd, c_ctx, w_ada, b_ada, ln_g, ln_b,
           gla_w_in, gla_w_gate, gla_b_gate, gla_norm_g, gla_w_out,
           ssd_w_in, ssd_conv_w, ssd_conv_b, ssd_dt_bias, ssd_a_log, ssd_d, ssd_norm_g, ssd_w_out,
           moe_w_router, moe_router_bias, moe_w_gate, moe_w_up, moe_w_down,
           shared_w_gate, shared_w_up, shared_w_down):
    bp, tp, d = x_prompt.shape
    bs, ts, _ = x_sample.shape
    depth = w_ada.shape[0]
    n_prompt = bp * tp
    rows = Rows(n_prompt, ts, bs)
    alpha = (2 * depth) ** 0.25
    x = jnp.concatenate([x_prompt.reshape(n_prompt, d), x_sample.reshape(bs * ts, d)], axis=0)

    cond = jnp.zeros((8, d), f32).at[0].set(c_ctx).at[1:1 + bs].set(c)
    mods = ada_all(cond, w_ada, b_ada)[:, :1 + bs].reshape(depth * rows.mods_per_layer, 1, d)

    hk = gla_w_gate.shape[-1]
    hv = gla_w_out.shape[1]
    rope = _grid_rope(ts, hk // GLA_HEADS)
    di = ssd_w_out.shape[1]
    n_ssd_heads = ssd_dt_bias.shape[-1]
    conv_dim = ssd_conv_w.shape[-1]
    n_exp = moe_w_router.shape[-1]

    new_gla, new_ssd = [], []
    for l in range(depth):
        j = l // 2
        if l % 2 == 0:
            w_in = gla_w_in[j]
            n_main = 2 * hk + 2 * hv
            proj = mm_in(x, mods, rows, l, w_in, 0, n_main, 1024)
            w_lowrank = jnp.pad(w_in[:, n_main:], ((0, 0), (0, LANES - 2 * GLA_RANK)))
            g = mm_in(x, mods, rows, l, w_lowrank, 0, LANES, LANES)
            wg_pad = (jnp.zeros((2, LANES, hk), f32)
                      .at[0, :GLA_RANK].set(gla_w_gate[j, 0])
                      .at[1, GLA_RANK:2 * GLA_RANK].set(gla_w_gate[j, 1]))
            of_p, ob_p, st = gla_scan(proj, g, wg_pad, gla_b_gate[j], 0, bp, tp, None, None, True)
            of_s, ob_s = gla_scan(proj, g, wg_pad, gla_b_gate[j], n_prompt, bs, ts, rope, state_gla[:, j], False)
            y = jnp.concatenate([gla_gate(of_p, ob_p, proj, gla_norm_g[j], 2 * hk + hv, 0),
                                 gla_gate(of_s, ob_s, proj, gla_norm_g[j], 2 * hk + hv, n_prompt)], axis=0)
            w_out = gla_w_out[j].astype(bf16)
            new_gla.append(st)
        else:
            w_in = ssd_w_in[j]
            n_main = di + conv_dim
            proj = mm_in(x, mods, rows, l, w_in, 0, n_main, 1024)
            dt = mm_in(x, mods, rows, l, w_in, n_main, 2 * n_ssd_heads, 2 * n_ssd_heads)
            d_rep = jnp.repeat(ssd_d[j], SSD_HEADDIM)
            xbc_p = ssd_conv(proj, ssd_conv_w[j], ssd_conv_b[j], di, 0, bp, tp)
            xbc_s = ssd_conv(proj, ssd_conv_w[j], ssd_conv_b[j], di, n_prompt, bs, ts)
            yf_p, yb_p, st = ssd_scan(xbc_p, dt, ssd_dt_bias[j], ssd_a_log[j], 0, bp, tp, None, True)
            yf_s, yb_s = ssd_scan(xbc_s, dt, ssd_dt_bias[j], ssd_a_log[j], n_prompt, bs, ts, state_ssd[:, j], False)
            y = jnp.concatenate([ssd_gate(yf_p, yb_p, xbc_p, proj, d_rep, ssd_norm_g[j], 0),
                                 ssd_gate(yf_s, yb_s, xbc_s, proj, d_rep, ssd_norm_g[j], n_prompt)], axis=0)
            w_out = ssd_w_out[j].astype(bf16)
            new_ssd.append(st)
        x1 = mm_post(y, w_out, x, mods, rows, l, 2, ln_g[l, 0], ln_b[l, 0], alpha)

        h, idx_t, w_t = moe_router(x1, mods, rows, l, moe_w_router[l], moe_router_bias[l])
        order, blk_expert, blk_src, blk_rows = routing_tables(idx_t, n_exp, MOE_ROWS_PER_BLOCK)
        y_slabs = moe_experts(h, order, blk_expert, blk_src, blk_rows, moe_w_gate, moe_w_up, moe_w_down, l,
                              MOE_ROWS_PER_BLOCK)
        x = moe_ffn_post(h, x1, y_slabs, w_t, shared_w_gate[l].astype(bf16), shared_w_up[l].astype(bf16),
                         shared_w_down[l].astype(bf16), mods, rows, l, ln_g[l, 1], ln_b[l, 1], alpha)

    y_prompt = x[:n_prompt].reshape(bp, tp, d)
    y_sample = x[n_prompt:].reshape(bs, ts, d)
    return y_prompt, y_sample, jnp.stack(new_gla, axis=1), jnp.stack(new_ssd, axis=1)
```

```python
import functools
import math

import jax
import jax.numpy as jnp
from jax import lax
from jax.experimental import pallas as pl
from jax.experimental.pallas import tpu as pltpu

f32 = jnp.float32
bf16 = jnp.bfloat16

GRID_W = 64
GLA_HEADS = 4
GLA_RANK = 16
GLA_TAU = 16.0
GLA_CHUNK = 32
ROPE_THETA = 10000.0
SSD_HEADDIM = 64
SSD_STATE = 128
SSD_GROUPS = 8
N_EXPERT_GROUPS = 8
TOPK_GROUPS = 4
TOP_K = 8
ROUTED_SCALE = 2.5
LN_EPS = 1e-5
RMS_EPS = 1e-6

LANES = 128
VMEM_BYTES_V7X = 64 * 2 ** 20


def _cparams(n_axes, vmem_mib):
    assert vmem_mib * 2 ** 20 < VMEM_BYTES_V7X
    return pltpu.CompilerParams(dimension_semantics=("arbitrary",) * n_axes,
                                vmem_limit_bytes=vmem_mib * 2 ** 20)


def _dot(a, b):
    return jnp.dot(a, b, preferred_element_type=f32)


def _dot_nt(a, b):
    return lax.dot_general(a, b, (((1,), (1,)), ((), ())), preferred_element_type=f32)


def _dot_tn(a, b):
    return lax.dot_general(a, b, (((0,), (0,)), ((), ())), preferred_element_type=f32)


def _split3(x):
    hi = x.astype(bf16)
    r = x - hi.astype(f32)
    mid = r.astype(bf16)
    lo = (r - mid.astype(f32)).astype(bf16)
    return hi, mid, lo


def _dot_exact_lhs(a_bf, x):
    hi, mid, lo = _split3(x)
    return (_dot(a_bf, lo) + _dot(a_bf, mid)) + _dot(a_bf, hi)


def _dot_exact_rhs(x, e_bf):
    hi, mid, lo = _split3(x)
    return (_dot(lo, e_bf) + _dot(mid, e_bf)) + _dot(hi, e_bf)


def _silu(x):
    return x * jax.nn.sigmoid(x)


def _cast_rows(src_ref, dst_ref, rows):
    n = src_ref.shape[0]
    rows = math.gcd(n, rows)

    def body(i, c):
        r = pl.multiple_of(i * rows, rows)
        dst_ref[pl.ds(r, rows), :] = src_ref[pl.ds(r, rows), :].astype(dst_ref.dtype)
        return c

    lax.fori_loop(0, n // rows, body, 0)


def _ada_body(c_ref, w_ref, b_ref, o_ref):
    a = _silu(c_ref[...]).astype(bf16)
    o_ref[0] = _dot(a, w_ref[0].astype(bf16)) + b_ref[0]


def ada_all(cond8, w_ada, b_ada, tn=1024):
    L, D, N = w_ada.shape
    return pl.pallas_call(
        _ada_body,
        grid=(L, N // tn),
        in_specs=[pl.BlockSpec((8, D), lambda l, n: (0, 0)),
                  pl.BlockSpec((1, D, tn), lambda l, n: (l, 0, n)),
                  pl.BlockSpec((1, 1, tn), lambda l, n: (l, 0, n))],
        out_specs=pl.BlockSpec((1, 8, tn), lambda l, n: (l, 0, n)),
        out_shape=jax.ShapeDtypeStruct((L, 8, N), f32),
        compiler_params=_cparams(2, 40),
        name="ada",
    )(cond8, w_ada, b_ada.reshape(L, 1, N))


N_ADA = 6


class Rows:
    def __init__(self, n_prompt, seq_s, n_seq_s):
        self.n_prompt = n_prompt
        self.seq_s = seq_s
        self.mods_per_layer = (1 + n_seq_s) * N_ADA

    def group(self, row0):
        return jnp.where(row0 < self.n_prompt, 0, 1 + (row0 - self.n_prompt) // self.seq_s)


def _mod_index(rows, layer, j, tm):
    assert rows.n_prompt % tm == 0 and rows.seq_s % tm == 0

    def index_map(i):
        return (layer * rows.mods_per_layer + rows.group(i * tm) * N_ADA + j, 0, 0)
    return index_map


def _mm_in_body(x_ref, sh_ref, sc_ref, w_ref, o_ref, wb_ref):
    @pl.when(pl.program_id(1) == 0)
    def _():
        _cast_rows(w_ref, wb_ref, 256)

    h = (x_ref[...] * (1.0 + sc_ref[0]) + sh_ref[0]).astype(bf16)
    o_ref[...] = _dot(h, wb_ref[...])


def mm_in(x, mods, rows, layer, w, col0, n_cols, tn, tm=512):
    nt, d = x.shape
    assert n_cols % tn == 0 and col0 % tn == 0 and nt % tm == 0
    sh = _mod_index(rows, layer, 0, tm)
    sc = _mod_index(rows, layer, 1, tm)
    cb = col0 // tn
    return pl.pallas_call(
        _mm_in_body,
        grid=(n_cols // tn, nt // tm),
        in_specs=[pl.BlockSpec((tm, d), lambda n, m: (m, 0)),
                  pl.BlockSpec((1, 1, d), lambda n, m: sh(m)),
                  pl.BlockSpec((1, 1, d), lambda n, m: sc(m)),
                  pl.BlockSpec((d, tn), lambda n, m: (0, cb + n))],
        out_specs=pl.BlockSpec((tm, tn), lambda n, m: (m, n)),
        out_shape=jax.ShapeDtypeStruct((nt, n_cols), f32),
        scratch_shapes=[pltpu.VMEM((d, tn), bf16)],
        compiler_params=_cparams(2, 48),
        name="mm_in",
    )(x, mods, mods, w)


def _layer_norm(v, g, b):
    mu = jnp.mean(v, axis=-1, keepdims=True)
    d = v - mu
    var = jnp.mean(d * d, axis=-1, keepdims=True)
    return d * lax.rsqrt(var + LN_EPS) * g + b


def _mm_post_body(y_ref, w_ref, x_ref, gate_ref, g_ref, b_ref, o_ref, acc_ref, *, nk, alpha):
    k = pl.program_id(1)
    part = _dot(y_ref[...], w_ref[...])

    @pl.when(k == 0)
    def _():
        acc_ref[...] = part

    @pl.when(k > 0)
    def _():
        acc_ref[...] += part

    @pl.when(k == nk - 1)
    def _():
        v = alpha * x_ref[...] + gate_ref[0] * acc_ref[...]
        o_ref[...] = _layer_norm(v, g_ref[...], b_ref[...])


def mm_post(y, w_bf, x, mods, rows, layer, gate_j, ln_g, ln_b, alpha, tm=256, tk=2048):
    nt, kdim = y.shape
    d = x.shape[1]
    nk = kdim // tk
    gate = _mod_index(rows, layer, gate_j, tm)
    return pl.pallas_call(
        functools.partial(_mm_post_body, nk=nk, alpha=alpha),
        grid=(nt // tm, nk),
        in_specs=[pl.BlockSpec((tm, tk), lambda m, k: (m, k)),
                  pl.BlockSpec((tk, d), lambda m, k: (k, 0)),
                  pl.BlockSpec((tm, d), lambda m, k: (m, 0)),
                  pl.BlockSpec((1, 1, d), lambda m, k: gate(m)),
                  pl.BlockSpec((1, d), lambda m, k: (0, 0)),
                  pl.BlockSpec((1, d), lambda m, k: (0, 0))],
        out_specs=pl.BlockSpec((tm, d), lambda m, k: (m, 0)),
        out_shape=jax.ShapeDtypeStruct((nt, d), f32),
        scratch_shapes=[pltpu.VMEM((tm, d), f32)],
        compiler_params=_cparams(2, 48),
        name="mm_post",
    )(y, w_bf, x, mods, ln_g.reshape(1, d), ln_b.reshape(1, d))


def _chunk_masks(tb, c, reverse):
    i = lax.broadcasted_iota(jnp.int32, (tb, tb), 0)
    j = lax.broadcasted_iota(jnp.int32, (tb, tb), 1)
    same = (i // c) == (j // c)
    tri = (j >= i) if reverse else (j <= i)
    return jnp.logical_and(same, tri), same


def _gla_direction(q, k, v, g, wg, bg, rope, st_ref, reverse, c):
    tb, dk = q.shape
    assert tb % (2 * c) == 0
    causal, same = _chunk_masks(tb, c, reverse)
    causal_bf = jnp.where(causal, 1.0, 0.0).astype(bf16)
    same_bf = jnp.where(same, 1.0, 0.0).astype(bf16)

    pre = _dot(g.astype(bf16), wg.astype(bf16)) + bg
    log_a = (jnp.minimum(pre, 0.0) - jnp.log1p(jnp.exp(-jnp.abs(pre)))) * (1.0 / GLA_TAU)
    b = _dot_exact_lhs(causal_bf, log_a)
    total = _dot_exact_lhs(same_bf, log_a)

    q = q * (dk ** -0.5)
    if rope is not None:
        cos, sin = rope
        half = dk // 2
        q = jnp.concatenate([q[:, :half] * cos - q[:, half:] * sin, q[:, :half] * sin + q[:, half:] * cos], axis=1)
        k = jnp.concatenate([k[:, :half] * cos - k[:, half:] * sin, k[:, :half] * sin + k[:, half:] * cos], axis=1)

    q_loc = q * jnp.exp(b)
    k_loc = k * jnp.exp(total - b)
    q_dec = q_loc.astype(bf16)
    k_inv = (k * jnp.exp(-b)).astype(bf16)
    k_end = k_loc.astype(bf16)
    v_bf = v.astype(bf16)

    pair = 2 * c
    n_pairs = tb // pair
    sub = lax.broadcasted_iota(jnp.int32, (tb, dk), 0) // c
    scanned_first = (sub % 2) == (1 if reverse else 0)
    total_other = jnp.where(sub % 2 == 0, pltpu.roll(total, tb - c, 0), pltpu.roll(total, c, 0))
    e_other = jnp.exp(total_other)
    q_pair = (q_loc * jnp.where(scanned_first, 1.0, e_other)).astype(bf16)
    k_pair = (k_loc * jnp.where(scanned_first, e_other, 1.0)).astype(bf16)
    dec_pair = jnp.exp(total + total_other)

    i = lax.broadcasted_iota(jnp.int32, (tb, tb), 0)
    j = lax.broadcasted_iota(jnp.int32, (tb, tb), 1)
    i_second = ((i // c) % 2) == (0 if reverse else 1)
    cross = jnp.logical_and(jnp.logical_and((i // pair) == (j // pair), (i // c) != (j // c)), i_second)
    scores = jnp.where(causal, _dot_nt(q_dec, k_inv), 0.0) + jnp.where(cross, _dot_nt(q_dec, k_end), 0.0)
    o_intra = _dot(scores.astype(bf16), v_bf)

    order = range(n_pairs - 1, -1, -1) if reverse else range(n_pairs)
    o_inter = [None] * n_pairs
    st = st_ref[...]
    for pi in order:
        sl = slice(pi * pair, (pi + 1) * pair)
        o_inter[pi] = _dot_nt(q_pair[sl], st.astype(bf16))
        st = st * dec_pair[pi * pair:pi * pair + 1, :] + _dot_tn(v_bf[sl], k_pair[sl])
    st_ref[...] = st
    return o_intra + jnp.concatenate(o_inter, axis=0)


def _gla_body(*refs, c, use_rope, has_init, emit_state, nb):
    refs = list(refs)
    qf, kf, vf, gf, qb, kb, vb, gb, wgf, wgb, bgf, bgb = refs[:12]
    refs = refs[12:]
    rope_f = rope_b = None
    if use_rope:
        rope_f = (refs[0][...], refs[1][...])
        rope_b = (refs[2][...], refs[3][...])
        refs = refs[4:]
    if has_init:
        s0_ref = refs[0]
        refs = refs[1:]
    of_ref, ob_ref = refs[:2]
    refs = refs[2:]
    if emit_state:
        sfin_ref = refs[0]
        refs = refs[1:]
    stf_ref, stb_ref = refs
    t = pl.program_id(2)

    @pl.when(t == 0)
    def _():
        if has_init:
            stf_ref[...] = s0_ref[0, 0, 0].T
            stb_ref[...] = s0_ref[0, 1, 0].T
        else:
            stf_ref[...] = jnp.zeros_like(stf_ref)
            stb_ref[...] = jnp.zeros_like(stb_ref)

    of_ref[...] = _gla_direction(qf[...], kf[...], vf[...], gf[...], wgf[...], bgf[0], rope_f, stf_ref, False, c)
    ob_ref[...] = _gla_direction(qb[...], kb[...], vb[...], gb[...], wgb[...], bgb[0], rope_b, stb_ref, True, c)

    if emit_state:
        @pl.when(t == nb - 1)
        def _():
            sfin_ref[0, 0, 0] = stf_ref[...].T
            sfin_ref[0, 1, 0] = stb_ref[...].T


def gla_scan(proj, g, wg_pad, b_gate, row0, n_seq, seq_len, rope=None, s0=None, emit_state=False, tb=256):
    h = GLA_HEADS
    hk = wg_pad.shape[2]
    dk = hk // h
    dv = 2 * dk
    nb = seq_len // tb
    base = row0 // tb
    assert seq_len % tb == 0 and row0 % tb == 0 and dv == 2 * dk

    def fblk(s, t):
        return base + s * nb + t

    def bblk(s, t):
        return base + s * nb + (nb - 1 - t)

    v_off = 2 * hk // dv
    in_specs, args = [], []
    for blk in (fblk, bblk):
        in_specs += [pl.BlockSpec((tb, dk), lambda s, hh, t, blk=blk: (blk(s, t), hh)),
                     pl.BlockSpec((tb, dk), lambda s, hh, t, blk=blk: (blk(s, t), h + hh)),
                     pl.BlockSpec((tb, dv), lambda s, hh, t, blk=blk: (blk(s, t), v_off + hh)),
                     pl.BlockSpec((tb, LANES), lambda s, hh, t, blk=blk: (blk(s, t), 0))]
        args += [proj, proj, proj, g]
    in_specs += [pl.BlockSpec((None, LANES, dk), lambda s, hh, t: (0, 0, hh)),
                 pl.BlockSpec((None, LANES, dk), lambda s, hh, t: (1, 0, hh)),
                 pl.BlockSpec((None, 1, dk), lambda s, hh, t: (0, 0, hh)),
                 pl.BlockSpec((None, 1, dk), lambda s, hh, t: (1, 0, hh))]
    bg3 = b_gate.reshape(2, 1, hk)
    args += [wg_pad, wg_pad, bg3, bg3]
    if rope is not None:
        for tab_blk in (lambda s, hh, t: (t, 0), lambda s, hh, t: (nb - 1 - t, 0)):
            in_specs += [pl.BlockSpec((tb, dk // 2), tab_blk), pl.BlockSpec((tb, dk // 2), tab_blk)]
            args += [rope[0], rope[1]]
    if s0 is not None:
        in_specs.append(pl.BlockSpec((1, 2, 1, dk, dv), lambda s, hh, t: (s, 0, hh, 0, 0)))
        args.append(s0)
    n_rows = n_seq * seq_len
    out_specs = [pl.BlockSpec((tb, dv), lambda s, hh, t: (s * nb + t, hh)),
                 pl.BlockSpec((tb, dv), lambda s, hh, t: (s * nb + (nb - 1 - t), hh))]
    out_shape = [jax.ShapeDtypeStruct((n_rows, h * dv), f32)] * 2
    if emit_state:
        out_specs.append(pl.BlockSpec((1, 2, 1, dk, dv), lambda s, hh, t: (s, 0, hh, 0, 0)))
        out_shape.append(jax.ShapeDtypeStruct((n_seq, 2, h, dk, dv), f32))
    return pl.pallas_call(
        functools.partial(_gla_body, c=GLA_CHUNK, use_rope=rope is not None, has_init=s0 is not None,
                          emit_state=emit_state, nb=nb),
        grid=(n_seq, h, nb),
        in_specs=in_specs,
        out_specs=out_specs,
        out_shape=out_shape,
        scratch_shapes=[pltpu.VMEM((dv, dk), f32), pltpu.VMEM((dv, dk), f32)],
        compiler_params=_cparams(3, 40),
        name="gla_scan",
    )(*args)


def _gla_gate_body(of_ref, ob_ref, r_ref, g_ref, y_ref):
    o = of_ref[...] + ob_ref[...]
    o = o * lax.rsqrt(jnp.mean(o * o, axis=-1, keepdims=True) + RMS_EPS) * g_ref[...]
    y_ref[...] = (o * _silu(r_ref[...])).astype(y_ref.dtype)


def gla_gate(o_f, o_b, proj, norm_g, r_col0, row0, tm=512):
    nt, hv = o_f.shape
    dv = norm_g.shape[0]
    r_off = r_col0 // dv
    rb = row0 // tm
    assert row0 % tm == 0 and nt % tm == 0
    return pl.pallas_call(
        _gla_gate_body,
        grid=(nt // tm, hv // dv),
        in_specs=[pl.BlockSpec((tm, dv), lambda m, hh: (m, hh)),
                  pl.BlockSpec((tm, dv), lambda m, hh: (m, hh)),
                  pl.BlockSpec((tm, dv), lambda m, hh: (rb + m, r_off + hh)),
                  pl.BlockSpec((1, dv), lambda m, hh: (0, 0))],
        out_specs=pl.BlockSpec((tm, dv), lambda m, hh: (m, hh)),
        out_shape=jax.ShapeDtypeStruct((nt, hv), bf16),
        compiler_params=_cparams(2, 32),
        name="gla_gate",
    )(o_f, o_b, proj, norm_g.reshape(1, dv))


def _conv_body(x_ref, w_ref, b_ref, o_ref, *, seq_len):
    x = x_ref[...]
    t = x.shape[0]
    pos = lax.broadcasted_iota(jnp.int32, x.shape, 0) % seq_len
    prev = jnp.where(pos == 0, 0.0, pltpu.roll(x, 1, 0))
    nxt = jnp.where(pos == seq_len - 1, 0.0, pltpu.roll(x, t - 1, 0))
    w = w_ref[...]
    o_ref[...] = _silu(prev * w[0:1] + x * w[1:2] + nxt * w[2:3] + b_ref[...])


CONV_BLOCK_ROWS = 4096


def ssd_conv(proj, conv_w, conv_b, col0, row0, n_seq, seq_len, tc=4 * LANES):
    nc = conv_w.shape[1]
    n_rows = n_seq * seq_len
    tr = max(seq_len, math.gcd(n_rows, CONV_BLOCK_ROWS))
    assert col0 % tc == 0 and nc % tc == 0 and tr % seq_len == 0 and n_rows % tr == 0 and row0 % tr == 0
    cb, rb = col0 // tc, row0 // tr
    return pl.pallas_call(
        functools.partial(_conv_body, seq_len=seq_len),
        grid=(n_rows // tr, nc // tc),
        in_specs=[pl.BlockSpec((tr, tc), lambda s, ci: (rb + s, cb + ci)),
                  pl.BlockSpec((3, tc), lambda s, ci: (0, ci)),
                  pl.BlockSpec((1, tc), lambda s, ci: (0, ci))],
        out_specs=pl.BlockSpec((tr, tc), lambda s, ci: (s, ci)),
        out_shape=jax.ShapeDtypeStruct((n_seq * seq_len, nc), f32),
        compiler_params=_cparams(2, 48),
        name="ssd_conv",
    )(proj, conv_w, conv_b.reshape(1, nc))


def _softplus(x):
    return jnp.maximum(x, 0.0) + jnp.log1p(jnp.exp(-jnp.abs(x)))


def _ssd_direction(x, bm, cm, dt_raw, bias, a_row, e_wide, e_head, ht_ref, reverse):
    tb, width = x.shape
    p = SSD_HEADDIM
    n_heads = width // p
    i = lax.broadcasted_iota(jnp.int32, (tb, tb), 0)
    j = lax.broadcasted_iota(jnp.int32, (tb, tb), 1)
    causal = (j >= i) if reverse else (j <= i)
    causal_bf = jnp.where(causal, 1.0, 0.0).astype(bf16)

    dt = _softplus(dt_raw + bias)
    cs = _dot_exact_lhs(causal_bf, dt * a_row)
    cs_w = _dot_exact_rhs(cs, e_wide)
    dt_w = _dot_exact_rhs(dt, e_wide)
    cs_h = _dot_exact_rhs(cs, e_head)
    cs_ht = cs_h.T
    last = 0 if reverse else tb - 1
    cs_last = cs_w[last:last + 1, :]

    bm_bf = bm.astype(bf16)
    cm_bf = cm.astype(bf16)
    cb = _dot_nt(cm_bf, bm_bf)
    xdt = (x * dt_w).astype(bf16)
    lane = lax.broadcasted_iota(jnp.int32, (tb, LANES), 1)
    heads_per_slab = LANES // p
    y_slabs = []
    for sidx in range(width // LANES):
        xs = xdt[:, sidx * LANES:(sidx + 1) * LANES]
        acc = None
        for hs in range(heads_per_slab):
            hh = sidx * heads_per_slab + hs
            diff = cs_h[:, hh:hh + 1] - cs_ht[hh:hh + 1, :]
            m = (cb * jnp.exp(jnp.where(causal, diff, -jnp.inf))).astype(bf16)
            part = _dot(m, jnp.where((lane // p) == hs, xs, jnp.zeros_like(xs)))
            acc = part if acc is None else acc + part
        y_slabs.append(acc)
    ht = ht_ref[...]
    y = jnp.concatenate(y_slabs, axis=1) + _dot(cm_bf, ht.astype(bf16)) * jnp.exp(cs_w)
    xw = (x * (jnp.exp(cs_last - cs_w) * dt_w)).astype(bf16)
    ht_ref[...] = ht * jnp.exp(cs_last) + _dot_tn(bm_bf, xw)
    return y


def _ssd_body(*refs, has_init, emit_state, nb):
    refs = list(refs)
    xf, bf_, cf, dtf, xb, bb, cb_, dtb, bias, alog, ewf, ewb, ehf, ehb = refs[:14]
    refs = refs[14:]
    if has_init:
        s0_ref = refs[0]
        refs = refs[1:]
    yf_ref, yb_ref = refs[:2]
    refs = refs[2:]
    if emit_state:
        sfin_ref = refs[0]
        refs = refs[1:]
    htf_ref, htb_ref = refs
    t = pl.program_id(2)
    n, width = htf_ref.shape

    @pl.when(t == 0)
    def _():
        if has_init:
            htf_ref[...] = s0_ref[0, 0].reshape(width, n).T
            htb_ref[...] = s0_ref[0, 1].reshape(width, n).T
        else:
            htf_ref[...] = jnp.zeros_like(htf_ref)
            htb_ref[...] = jnp.zeros_like(htb_ref)

    a_row = -jnp.exp(alog[...])
    yf_ref[...] = _ssd_direction(xf[...], bf_[...], cf[...], dtf[...], bias[...], a_row, ewf[...], ehf[...],
                                 htf_ref, False)
    yb_ref[...] = _ssd_direction(xb[...], bb[...], cb_[...], dtb[...], bias[...], a_row, ewb[...], ehb[...],
                                 htb_ref, True)

    if emit_state:
        @pl.when(t == nb - 1)
        def _():
            sfin_ref[0, 0] = htf_ref[...].T.reshape(sfin_ref.shape[2:])
            sfin_ref[0, 1] = htb_ref[...].T.reshape(sfin_ref.shape[2:])


def _ssd_select_matrices(n_heads):
    hg = n_heads // SSD_GROUPS
    col = jnp.arange(2 * n_heads)[None, None, :, None]
    d = jnp.arange(2)[:, None, None, None]
    g = jnp.arange(SSD_GROUPS)[None, :, None, None]
    wide = jnp.arange(hg * SSD_HEADDIM)[None, None, None, :] // SSD_HEADDIM
    head = jnp.arange(LANES)[None, None, None, :]
    first = d * n_heads + g * hg
    e_wide = (col == first + wide).astype(bf16)
    e_head = jnp.logical_and(col == first + head, head < hg).astype(bf16)
    return e_wide, e_head


def ssd_scan(xbc, dt, dt_bias, a_log, row0, n_seq, seq_len, s0=None, emit_state=False, tb=256):
    n_heads = dt_bias.shape[1]
    assert 2 * n_heads == LANES
    hg = n_heads // SSD_GROUPS
    p, n = SSD_HEADDIM, SSD_STATE
    width = hg * p
    di = n_heads * p
    nb = seq_len // tb
    base = row0 // tb
    assert seq_len % tb == 0 and row0 % tb == 0 and n == LANES
    e_wide, e_head = _ssd_select_matrices(n_heads)
    b_off = di // n
    c_off = b_off + SSD_GROUPS

    in_specs, args = [], []
    for rev in (False, True):
        def blk(s, t, rev=rev):
            return s * nb + ((nb - 1 - t) if rev else t)
        in_specs += [pl.BlockSpec((tb, width), lambda s, g, t, blk=blk: (blk(s, t), g)),
                     pl.BlockSpec((tb, n), lambda s, g, t, blk=blk: (blk(s, t), b_off + g)),
                     pl.BlockSpec((tb, n), lambda s, g, t, blk=blk: (blk(s, t), c_off + g)),
                     pl.BlockSpec((tb, LANES), lambda s, g, t, blk=blk: (base + blk(s, t), 0))]
        args += [xbc, xbc, xbc, dt]
    in_specs += [pl.BlockSpec((1, LANES), lambda s, g, t: (0, 0)),
                 pl.BlockSpec((1, LANES), lambda s, g, t: (0, 0)),
                 pl.BlockSpec((None, None, LANES, width), lambda s, g, t: (0, g, 0, 0)),
                 pl.BlockSpec((None, None, LANES, width), lambda s, g, t: (1, g, 0, 0)),
                 pl.BlockSpec((None, None, LANES, LANES), lambda s, g, t: (0, g, 0, 0)),
                 pl.BlockSpec((None, None, LANES, LANES), lambda s, g, t: (1, g, 0, 0))]
    args += [dt_bias.reshape(1, LANES), a_log.reshape(1, LANES), e_wide, e_wide, e_head, e_head]
    if s0 is not None:
        in_specs.append(pl.BlockSpec((1, 2, hg, p, n), lambda s, g, t: (s, 0, g, 0, 0)))
        args.append(s0)
    n_rows = n_seq * seq_len
    out_specs = [pl.BlockSpec((tb, width), lambda s, g, t: (s * nb + t, g)),
                 pl.BlockSpec((tb, width), lambda s, g, t: (s * nb + (nb - 1 - t), g))]
    out_shape = [jax.ShapeDtypeStruct((n_rows, di), f32)] * 2
    if emit_state:
        out_specs.append(pl.BlockSpec((1, 2, hg, p, n), lambda s, g, t: (s, 0, g, 0, 0)))
        out_shape.append(jax.ShapeDtypeStruct((n_seq, 2, n_heads, p, n), f32))
    return pl.pallas_call(
        functools.partial(_ssd_body, has_init=s0 is not None, emit_state=emit_state, nb=nb),
        grid=(n_seq, SSD_GROUPS, nb),
        in_specs=in_specs,
        out_specs=out_specs,
        out_shape=out_shape,
        scratch_shapes=[pltpu.VMEM((n, width), f32), pltpu.VMEM((n, width), f32)],
        compiler_params=_cparams(3, 40),
        name="ssd_scan",
    )(*args)


def _ssd_gate_body(yf_ref, yb_ref, xs_ref, z_ref, d_ref, g_ref, o_ref):
    y = yf_ref[...] + yb_ref[...] + d_ref[...] * xs_ref[...]
    y = y * _silu(z_ref[...])
    y = y * lax.rsqrt(jnp.mean(y * y, axis=-1, keepdims=True) + RMS_EPS)
    o_ref[...] = (y * g_ref[...]).astype(o_ref.dtype)


def ssd_gate(y_f, y_b, xbc, proj, d_rep, norm_g, row0, tm=512):
    nt, di = y_f.shape
    gw = di // SSD_GROUPS
    rb = row0 // tm
    assert row0 % tm == 0 and nt % tm == 0
    return pl.pallas_call(
        _ssd_gate_body,
        grid=(nt // tm, SSD_GROUPS),
        in_specs=[pl.BlockSpec((tm, gw), lambda m, gg: (m, gg)),
                  pl.BlockSpec((tm, gw), lambda m, gg: (m, gg)),
                  pl.BlockSpec((tm, gw), lambda m, gg: (m, gg)),
                  pl.BlockSpec((tm, gw), lambda m, gg: (rb + m, gg)),
                  pl.BlockSpec((1, gw), lambda m, gg: (0, gg)),
                  pl.BlockSpec((1, gw), lambda m, gg: (0, gg))],
        out_specs=pl.BlockSpec((tm, gw), lambda m, gg: (m, gg)),
        out_shape=jax.ShapeDtypeStruct((nt, di), bf16),
        compiler_params=_cparams(2, 32),
        name="ssd_gate",
    )(y_f, y_b, xbc, proj, d_rep.reshape(1, di), norm_g.reshape(1, di))


def _to_slabs(v, slab_ref):
    rows, d = v.shape
    n_slab = d // LANES
    for s in range(n_slab):
        slab_ref[pl.ds(s, rows, stride=n_slab), :] = v[:, s * LANES:(s + 1) * LANES]


def _from_slabs(slab_ref, rows):
    n_slab = slab_ref.shape[0] // rows
    return jnp.concatenate([slab_ref[pl.ds(s, rows, stride=n_slab), :] for s in range(n_slab)], axis=1)


def _router_body(x_ref, sh_ref, sc_ref, w_ref, b_ref, h_ref, idx_ref, wt_ref, *, n_exp):
    h = x_ref[...] * (1.0 + sc_ref[0]) + sh_ref[0]
    _to_slabs(h, h_ref)
    tm = h.shape[0]
    scores = jax.nn.sigmoid(_dot(h.astype(bf16), w_ref[...].astype(bf16)))
    choice = scores + b_ref[...]
    n_grp = N_EXPERT_GROUPS
    per_grp = n_exp // n_grp
    s3 = scores.T[:n_exp].reshape(n_grp, per_grp, tm)
    c3 = choice.T[:n_exp].reshape(n_grp, per_grp, tm)
    ninf = -jnp.inf
    pos = lax.broadcasted_iota(jnp.int32, c3.shape, 1)
    grp = lax.broadcasted_iota(jnp.int32, c3.shape, 0)
    eidx = grp * per_grp + pos

    m1 = jnp.max(c3, axis=1, keepdims=True)
    i1 = jnp.min(jnp.where(c3 == m1, pos, per_grp), axis=1, keepdims=True)
    m2 = jnp.max(jnp.where(pos == i1, ninf, c3), axis=1, keepdims=True)
    gscore = m1 + m2
    gpos = lax.broadcasted_iota(jnp.int32, gscore.shape, 0)
    keep = jnp.zeros(gscore.shape, jnp.bool_)
    for _ in range(TOPK_GROUPS):
        m = jnp.max(gscore, axis=0, keepdims=True)
        gi = jnp.min(jnp.where(gscore == m, gpos, n_grp), axis=0, keepdims=True)
        hit = gpos == gi
        keep = jnp.logical_or(keep, hit)
        gscore = jnp.where(hit, ninf, gscore)

    cur = jnp.where(keep, c3, ninf)
    picked_w = []
    for k in range(TOP_K):
        m = jnp.max(jnp.max(cur, axis=1, keepdims=True), axis=0, keepdims=True)
        ei = jnp.min(jnp.min(jnp.where(cur == m, eidx, n_exp), axis=1, keepdims=True), axis=0, keepdims=True)
        hit = eidx == ei
        wk = jnp.sum(jnp.sum(jnp.where(hit, s3, 0.0), axis=1, keepdims=True), axis=0, keepdims=True)
        cur = jnp.where(hit, ninf, cur)
        idx_ref[k:k + 1, :] = ei.reshape(1, tm)
        picked_w.append(wk.reshape(1, tm))
    total = picked_w[0]
    for wk in picked_w[1:]:
        total = total + wk
    for k in range(TOP_K):
        wt_ref[k:k + 1, :] = picked_w[k] / total * ROUTED_SCALE


def moe_router(x, mods, rows, layer, w_router, router_bias, tm=512):
    nt, d = x.shape
    n_exp = w_router.shape[1]
    n_slab = d // LANES
    assert n_exp <= LANES and n_exp % N_EXPERT_GROUPS == 0 and d % LANES == 0
    w_pad = jnp.pad(w_router, ((0, 0), (0, LANES - n_exp)))
    b_pad = jnp.pad(router_bias, (0, LANES - n_exp)).reshape(1, LANES)
    sh = _mod_index(rows, layer, 3, tm)
    sc = _mod_index(rows, layer, 4, tm)
    return pl.pallas_call(
        functools.partial(_router_body, n_exp=n_exp),
        grid=(nt // tm,),
        in_specs=[pl.BlockSpec((tm, d), lambda m: (m, 0)),
                  pl.BlockSpec((1, 1, d), sh),
                  pl.BlockSpec((1, 1, d), sc),
                  pl.BlockSpec((d, LANES), lambda m: (0, 0)),
                  pl.BlockSpec((1, LANES), lambda m: (0, 0))],
        out_specs=[pl.BlockSpec((tm * n_slab, LANES), lambda m: (m, 0)),
                   pl.BlockSpec((TOP_K, tm), lambda m: (0, m)),
                   pl.BlockSpec((TOP_K, tm), lambda m: (0, m))],
        out_shape=[jax.ShapeDtypeStruct((nt * n_slab, LANES), f32),
                   jax.ShapeDtypeStruct((TOP_K, nt), jnp.int32),
                   jax.ShapeDtypeStruct((TOP_K, nt), f32)],
        compiler_params=_cparams(1, 32),
        name="moe_router",
    )(x, mods, mods, w_pad, b_pad)


def routing_tables(idx_t, n_exp, tmx):
    nt = idx_t.shape[1]
    tk = TOP_K * nt
    e_flat = idx_t.reshape(tk)
    _, order = lax.sort_key_val(e_flat, jnp.arange(tk, dtype=jnp.int32))
    experts = jnp.arange(n_exp, dtype=jnp.int32)
    counts = jnp.sum((e_flat[None, :] == experts[:, None]).astype(jnp.int32), axis=1)
    start = jnp.cumsum(counts) - counts
    n_blk = (counts + tmx - 1) // tmx
    blk_end = jnp.cumsum(n_blk)
    n_blocks = -(-(tk + n_exp * (tmx - 1)) // tmx)
    b = jnp.arange(n_blocks, dtype=jnp.int32)
    blk_expert = jnp.minimum(jnp.sum((blk_end[None, :] <= b[:, None]).astype(jnp.int32), axis=1), n_exp - 1)
    onehot = (blk_expert[:, None] == experts[None, :]).astype(jnp.int32)
    pick = lambda v: jnp.sum(onehot * v[None, :], axis=1)
    offset = (b - pick(blk_end - n_blk)) * tmx
    blk_rows = jnp.clip(pick(counts) - offset, 0, tmx)
    blk_src = jnp.clip(pick(start) + offset, 0, tk - 1)
    return order, blk_expert, blk_src, blk_rows


def _slab_copy(src_ref, src_slab, dst_ref, dst_slab, n_slab, sem):
    src = src_ref.at[pl.ds(pl.multiple_of(src_slab * n_slab, n_slab), n_slab), :]
    dst = dst_ref.at[pl.ds(pl.multiple_of(dst_slab * n_slab, n_slab), n_slab), :]
    return pltpu.make_async_copy(src, dst, sem)


def _expert_body(order_ref, be_ref, src_ref, rows_ref, h_hbm, wg_ref, wu_ref, wd_ref, y_hbm,
                 xbuf, ybuf, xs_ref, wgb, wub, wdb, gsem, ssem, *, tmx, nt, n_slab):
    i = pl.program_id(0)
    n = pl.num_programs(0)
    tk = order_ref.shape[0]
    slot = i % 2
    used = rows_ref[i] > 0

    def gather_copy(base, r, s):
        a = order_ref[jnp.minimum(base + r, tk - 1)]
        return _slab_copy(h_hbm, lax.rem(a, nt), xbuf.at[s], r, n_slab, gsem.at[s])

    def scatter_copy(base, n_real, r, s):
        a = jnp.where(r < n_real, order_ref[jnp.minimum(base + r, tk - 1)], tk + s * tmx + r)
        return _slab_copy(ybuf.at[s], r, y_hbm, a, n_slab, ssem.at[s])

    def wait_gather(s):
        pltpu.make_async_copy(h_hbm.at[pl.ds(0, tmx * n_slab), :], xbuf.at[s], gsem.at[s]).wait()

    def wait_scatter(s):
        pltpu.make_async_copy(ybuf.at[s], y_hbm.at[pl.ds(0, tmx * n_slab), :], ssem.at[s]).wait()

    @pl.when(i == 0)
    def _():
        ybuf[...] = jnp.zeros_like(ybuf)
        pltpu.make_async_copy(ybuf.at[0], y_hbm.at[pl.ds(tk * n_slab, tmx * n_slab), :], ssem.at[0]).start()

        def issue(r, c):
            gather_copy(src_ref[0], r, 0).start()
            return c

        lax.fori_loop(0, tmx, issue, 0, unroll=8)

    first_of_expert = jnp.logical_or(i == 0, be_ref[i] != be_ref[jnp.maximum(i - 1, 0)])

    @pl.when(jnp.logical_and(used, first_of_expert))
    def _():
        _cast_rows(wg_ref, wgb, 256)
        _cast_rows(wu_ref, wub, 256)
        _cast_rows(wd_ref, wdb, 256)

    has_next = jnp.logical_and(i + 1 < n, rows_ref[jnp.minimum(i + 1, n - 1)] > 0)
    nxt = jnp.where(has_next, i + 1, i)
    prev = jnp.maximum(i - 1, 0)
    prev_rows = jnp.where(i > 0, rows_ref[prev], 0)

    @pl.when(used)
    def _():
        wait_gather(slot)
        xs_ref[...] = _from_slabs(xbuf.at[slot], tmx).astype(bf16)
        nxt_base = src_ref[nxt]
        prev_base = src_ref[prev]
        for r in range(tmx):
            gather_copy(nxt_base, r, 1 - slot).start(priority=1)
        for r in range(tmx):
            scatter_copy(prev_base, prev_rows, r, 1 - slot).start(priority=0)
        x = xs_ref[...]
        hmid = (_silu(_dot(x, wgb[...])) * _dot(x, wub[...])).astype(bf16)
        out = _dot(hmid, wdb[...])
        wait_scatter(slot)
        _to_slabs(out, ybuf.at[slot])

    @pl.when(jnp.logical_and(used, jnp.logical_not(has_next)))
    def _():
        wait_gather(1 - slot)
        wait_scatter(1 - slot)
        base, n_real = src_ref[i], rows_ref[i]

        def issue(r, c):
            scatter_copy(base, n_real, r, slot).start()
            return c

        lax.fori_loop(0, tmx, issue, 0, unroll=8)
        wait_scatter(slot)


def moe_experts(h_slabs, order, blk_expert, blk_src, blk_rows, w_gate, w_up, w_down, layer, tmx):
    n_blocks = blk_expert.shape[0]
    tk = order.shape[0]
    d, ff = w_gate.shape[2:]
    n_slab = d // LANES
    nt = h_slabs.shape[0] // n_slab
    wmap = lambda i, order, be, src, rows: (layer, be[i], 0, 0)
    grid_spec = pltpu.PrefetchScalarGridSpec(
        num_scalar_prefetch=4,
        grid=(n_blocks,),
        in_specs=[pl.BlockSpec(memory_space=pl.ANY),
                  pl.BlockSpec((None, None, d, ff), wmap),
                  pl.BlockSpec((None, None, d, ff), wmap),
                  pl.BlockSpec((None, None, ff, d), wmap)],
        out_specs=pl.BlockSpec(memory_space=pl.ANY),
        scratch_shapes=[pltpu.VMEM((2, tmx * n_slab, LANES), f32),
                        pltpu.VMEM((2, tmx * n_slab, LANES), f32),
                        pltpu.VMEM((tmx, d), bf16),
                        pltpu.VMEM((d, ff), bf16), pltpu.VMEM((d, ff), bf16), pltpu.VMEM((ff, d), bf16),
                        pltpu.SemaphoreType.DMA((2,)), pltpu.SemaphoreType.DMA((2,))],
    )
    return pl.pallas_call(
        functools.partial(_expert_body, tmx=tmx, nt=nt, n_slab=n_slab),
        grid_spec=grid_spec,
        out_shape=jax.ShapeDtypeStruct(((tk + 2 * tmx) * n_slab, LANES), f32),
        compiler_params=_cparams(1, 48),
        name="moe_experts",
    )(order, blk_expert, blk_src, blk_rows, h_slabs, w_gate, w_up, w_down)


def _ffn_post_body(*refs, alpha):
    h_ref = refs[0]
    y_refs = refs[1:1 + TOP_K]
    wrep_ref, sg_ref, su_ref, sd_ref, x_ref, gate_ref, g_ref, b_ref, o_ref, slab_scr = refs[1 + TOP_K:]
    tm = x_ref.shape[0]
    hb = _from_slabs(h_ref, tm).astype(bf16)
    hmid = (_silu(_dot(hb, sg_ref[...])) * _dot(hb, su_ref[...])).astype(bf16)
    ffn = _dot(hmid, sd_ref[...])
    wrep = wrep_ref[...]
    acc = None
    for k in range(TOP_K):
        part = y_refs[k][...] * wrep[:, k:k + 1]
        acc = part if acc is None else acc + part
    slab_scr[...] = acc
    v = alpha * x_ref[...] + gate_ref[0] * (_from_slabs(slab_scr, tm) + ffn)
    o_ref[...] = _layer_norm(v, g_ref[...], b_ref[...])


def moe_ffn_post(h_slabs, x, y_slabs, w_t, sg_bf, su_bf, sd_bf, mods, rows, layer, ln_g, ln_b, alpha, tm=128):
    nt, d = x.shape
    ff = sg_bf.shape[1]
    n_slab = d // LANES
    n_tiles = nt // tm
    w_rep = jnp.repeat(w_t.T, n_slab, axis=0)
    gate = _mod_index(rows, layer, 5, tm)
    slab_block = (tm * n_slab, LANES)
    y_specs = [pl.BlockSpec(slab_block, lambda m, k=k: (k * n_tiles + m, 0)) for k in range(TOP_K)]
    return pl.pallas_call(
        functools.partial(_ffn_post_body, alpha=alpha),
        grid=(n_tiles,),
        in_specs=[pl.BlockSpec(slab_block, lambda m: (m, 0))] + y_specs + [
                  pl.BlockSpec((tm * n_slab, TOP_K), lambda m: (m, 0)),
                  pl.BlockSpec((d, ff), lambda m: (0, 0)),
                  pl.BlockSpec((d, ff), lambda m: (0, 0)),
                  pl.BlockSpec((ff, d), lambda m: (0, 0)),
                  pl.BlockSpec((tm, d), lambda m: (m, 0)),
                  pl.BlockSpec((1, 1, d), gate),
                  pl.BlockSpec((1, d), lambda m: (0, 0)),
                  pl.BlockSpec((1, d), lambda m: (0, 0))],
        out_specs=pl.BlockSpec((tm, d), lambda m: (m, 0)),
        out_shape=jax.ShapeDtypeStruct((nt, d), f32),
        scratch_shapes=[pltpu.VMEM(slab_block, f32)],
        compiler_params=_cparams(1, 48),
        name="moe_ffn_post",
    )(h_slabs, *([y_slabs] * TOP_K), w_rep, sg_bf, su_bf, sd_bf, x, mods, ln_g.reshape(1, d), ln_b.reshape(1, d))


MOE_ROWS_PER_BLOCK = 256


def _grid_rope(seq_len, dk):
    t = jnp.arange(seq_len)
    row = (t // GRID_W).astype(f32)
    col = (t % GRID_W).astype(f32)
    nf = dk // 4
    freqs = ROPE_THETA ** (-jnp.arange(nf, dtype=f32) / nf)
    ang = jnp.concatenate([row[:, None] * freqs, col[:, None] * freqs], axis=-1)
    return jnp.cos(ang), jnp.sin(ang)


def kernel(x_prompt, x_sample, c, state_gla, state_ssd, c_ctx, w_ada, b_ada, ln_g, ln_b,
           gla_w_in, gla_w_gate, gla_b_gate, gla_norm_g, gla_w_out,
           ssd_w_in, ssd_conv_w, ssd_conv_b, ssd_dt_bias, ssd_a_log, ssd_d, ssd_norm_g, ssd_w_out,
           moe_w_router, moe_router_bias, moe_w_gate, moe_w_up, moe_w_down,
           shared_w_gate, shared_w_up, shared_w_down):
    bp, tp, d = x_prompt.shape
    bs, ts, _ = x_sample.shape
    depth = w_ada.shape[0]
    n_prompt = bp * tp
    rows = Rows(n_prompt, ts, bs)
    alpha = (2 * depth) ** 0.25
    x = jnp.concatenate([x_prompt.reshape(n_prompt, d), x_sample.reshape(bs * ts, d)], axis=0)

    cond = jnp.zeros((8, d), f32).at[0].set(c_ctx).at[1:1 + bs].set(c)
    mods = ada_all(cond, w_ada, b_ada)[:, :1 + bs].reshape(depth * rows.mods_per_layer, 1, d)

    hk = gla_w_gate.shape[-1]
    hv = gla_w_out.shape[1]
    rope = _grid_rope(ts, hk // GLA_HEADS)
    di = ssd_w_out.shape[1]
    n_ssd_heads = ssd_dt_bias.shape[-1]
    conv_dim = ssd_conv_w.shape[-1]
    n_exp = moe_w_router.shape[-1]

    new_gla, new_ssd = [], []
    for l in range(depth):
        j = l // 2
        if l % 2 == 0:
            w_in = gla_w_in[j]
            n_main = 2 * hk + 2 * hv
            proj = mm_in(x, mods, rows, l, w_in, 0, n_main, 1024)
            w_lowrank = jnp.pad(w_in[:, n_main:], ((0, 0), (0, LANES - 2 * GLA_RANK)))
            g = mm_in(x, mods, rows, l, w_lowrank, 0, LANES, LANES)
            wg_pad = (jnp.zeros((2, LANES, hk), f32)
                      .at[0, :GLA_RANK].set(gla_w_gate[j, 0])
                      .at[1, GLA_RANK:2 * GLA_RANK].set(gla_w_gate[j, 1]))
            of_p, ob_p, st = gla_scan(proj, g, wg_pad, gla_b_gate[j], 0, bp, tp, None, None, True)
            of_s, ob_s = gla_scan(proj, g, wg_pad, gla_b_gate[j], n_prompt, bs, ts, rope, state_gla[:, j], False)
            y = jnp.concatenate([gla_gate(of_p, ob_p, proj, gla_norm_g[j], 2 * hk + hv, 0),
                                 gla_gate(of_s, ob_s, proj, gla_norm_g[j], 2 * hk + hv, n_prompt)], axis=0)
            w_out = gla_w_out[j].astype(bf16)
            new_gla.append(st)
        else:
            w_in = ssd_w_in[j]
            n_main = di + conv_dim
            proj = mm_in(x, mods, rows, l, w_in, 0, n_main, 1024)
            dt = mm_in(x, mods, rows, l, w_in, n_main, 2 * n_ssd_heads, 2 * n_ssd_heads)
            d_rep = jnp.repeat(ssd_d[j], SSD_HEADDIM)
            xbc_p = ssd_conv(proj, ssd_conv_w[j], ssd_conv_b[j], di, 0, bp, tp)
            xbc_s = ssd_conv(proj, ssd_conv_w[j], ssd_conv_b[j], di, n_prompt, bs, ts)
            yf_p, yb_p, st = ssd_scan(xbc_p, dt, ssd_dt_bias[j], ssd_a_log[j], 0, bp, tp, None, True)
            yf_s, yb_s = ssd_scan(xbc_s, dt, ssd_dt_bias[j], ssd_a_log[j], n_prompt, bs, ts, state_ssd[:, j], False)
            y = jnp.concatenate([ssd_gate(yf_p, yb_p, xbc_p, proj, d_rep, ssd_norm_g[j], 0),
                                 ssd_gate(yf_s, yb_s, xbc_s, proj, d_rep, ssd_norm_g[j], n_prompt)], axis=0)
            w_out = ssd_w_out[j].astype(bf16)
            new_ssd.append(st)
        x1 = mm_post(y, w_out, x, mods, rows, l, 2, ln_g[l, 0], ln_b[l, 0], alpha)

        h, idx_t, w_t = moe_router(x1, mods, rows, l, moe_w_router[l], moe_router_bias[l])
        order, blk_expert, blk_src, blk_rows = routing_tables(idx_t, n_exp, MOE_ROWS_PER_BLOCK)
        y_slabs = moe_experts(h, order, blk_expert, blk_src, blk_rows, moe_w_gate, moe_w_up, moe_w_down, l,
                              MOE_ROWS_PER_BLOCK)
        x = moe_ffn_post(h, x1, y_slabs, w_t, shared_w_gate[l].astype(bf16), shared_w_up[l].astype(bf16),
                         shared_w_down[l].astype(bf16), mods, rows, l, ln_g[l, 1], ln_b[l, 1], alpha)

    y_prompt = x[:n_prompt].reshape(bp, tp, d)
    y_sample = x[n_prompt:].reshape(bs, ts, d)
    return y_prompt, y_sample, jnp.stack(new_gla, axis=1), jnp.stack(new_ssd, axis=1)
```
